```python
import jax
import jax.numpy as jnp
from jax import lax
import numpy as np

D_MODEL = 1024
BATCH = 16
SEQ = 4096
DEPTH = 1

CHUNK = 64
NORM_EPS = 1e-6
MIX_WIDTH = D_MODEL
POOL_WIDTH = MIX_WIDTH // 2
POOL_WINDOWS = (2, 4, 8, 16)
POOL_GROUPS = len(POOL_WINDOWS)
POOL_GROUP_DIM = POOL_WIDTH // POOL_GROUPS
GDN_HEAD_DIM = 128
GDN_WIDTH = MIX_WIDTH - POOL_WIDTH
GDN_HEADS = GDN_WIDTH // GDN_HEAD_DIM
CONV_WIDTH = 4
SPLIT_POINTS = (POOL_WIDTH, POOL_WIDTH + GDN_WIDTH, POOL_WIDTH + 2 * GDN_WIDTH, POOL_WIDTH + 3 * GDN_WIDTH, POOL_WIDTH + 4 * GDN_WIDTH, POOL_WIDTH + 4 * GDN_WIDTH + GDN_HEADS)
IN_PROJ_COLS = POOL_WIDTH + 4 * GDN_WIDTH + 2 * GDN_HEADS
PEER_HEADS = 8
PEER_N_KEYS = 128
PEER_N_EXPERTS = PEER_N_KEYS * PEER_N_KEYS
PEER_QUERY_DIM = 256
PEER_HALF = PEER_QUERY_DIM // 2
PEER_TOPK = 16
PEER_TOKEN_BLOCK = 128

kernel_name = "hybrid_pool_gdn_peer_block"


def rms_norm(x, gain):
    xf = x.astype(jnp.float32)
    y = xf * lax.rsqrt(jnp.mean(xf * xf, axis=-1, keepdims=True) + NORM_EPS)
    return (y * gain.astype(jnp.float32)).astype(x.dtype)


def l2_normalize(x):
    return x * lax.rsqrt(jnp.sum(x * x, axis=-1, keepdims=True) + NORM_EPS)


def multiscale_pool_mixer(xp, w_pool, b_pool, pool_scale):
    B, S, _ = xp.shape
    xg = xp.astype(jnp.float32).reshape(B, S, POOL_GROUPS, POOL_GROUP_DIM)
    cs = jnp.cumsum(xg, axis=1)
    pos = jnp.arange(S)
    outs = []
    for gi, win in enumerate(POOL_WINDOWS):
        c = cs[:, :, gi]
        prev = jnp.pad(c, ((0, 0), (win, 0), (0, 0)))[:, :S]
        count = jnp.minimum(pos + 1, win).astype(jnp.float32)[None, :, None]
        outs.append((c - prev) / count - xg[:, :, gi])
    d = jnp.stack(outs, axis=2)
    y = jnp.einsum('bsgc,gcd->bsgd', d, w_pool.astype(jnp.float32)) + b_pool.astype(jnp.float32)
    return (y.reshape(B, S, POOL_WIDTH) * pool_scale.astype(jnp.float32)).astype(xp.dtype)


def causal_depthwise_conv(x, w):
    C = x.shape[-1]
    return lax.conv_general_dilated(
        x, w[:, None, :].astype(x.dtype), window_strides=(1,),
        padding=((CONV_WIDTH - 1, 0),), dimension_numbers=('NWC', 'WIO', 'NWC'),
        feature_group_count=C)


def gated_delta_rule_chunked(q, k, v, g, beta):
    B, S, H, Dk = q.shape
    Dv = v.shape[-1]
    N = S // CHUNK

    def to_chunks(t):
        return t.reshape((B, N, CHUNK) + t.shape[2:]).swapaxes(2, 3)

    qc, kc, vc, gc, bc = (to_chunks(t) for t in (q, k, v, g, beta))
    gcum = jnp.cumsum(gc, axis=-1)
    idx = jnp.arange(CHUNK)
    causal = idx[:, None] >= idx[None, :]
    strict = idx[:, None] > idx[None, :]
    diff = gcum[..., :, None] - gcum[..., None, :]
    decay = jnp.where(causal, jnp.exp(jnp.where(causal, diff, 0.0)), 0.0)
    k_beta = kc * bc[..., None]
    v_beta = vc * bc[..., None]
    kk = jnp.einsum('bnhid,bnhjd->bnhij', k_beta, kc)
    a_mat = jnp.where(strict, kk * decay, 0.0) + jnp.eye(CHUNK, dtype=jnp.float32)
    rhs = jnp.concatenate([v_beta, k_beta * jnp.exp(gcum)[..., None]], axis=-1)
    sol = lax.linalg.triangular_solve(a_mat, rhs, left_side=True, lower=True, unit_diagonal=True)
    u = sol[..., :Dv]
    w = sol[..., Dv:]
    qk = jnp.einsum('bnhid,bnhjd->bnhij', qc, kc) * decay
    q_dec = qc * jnp.exp(gcum)[..., None]
    k_dec = kc * jnp.exp(gcum[..., -1:] - gcum)[..., None]
    chunk_decay = jnp.exp(gcum[..., -1])

    def step(state, inp):
        qk_n, qd_n, kd_n, u_n, w_n, cd_n = inp
        v_new = u_n - jnp.einsum('bhcd,bhde->bhce', w_n, state)
        o_n = jnp.einsum('bhcd,bhde->bhce', qd_n, state) + jnp.einsum('bhij,bhje->bhie', qk_n, v_new)
        state = state * cd_n[..., None, None] + jnp.einsum('bhcd,bhce->bhde', kd_n, v_new)
        return state, o_n

    xs = tuple(jnp.moveaxis(t, 1, 0) for t in (qk, q_dec, k_dec, u, w, chunk_decay))
    state0 = jnp.zeros((B, H, Dk, Dv), jnp.float32)
    _, o = lax.scan(step, state0, xs)
    return jnp.moveaxis(o, 0, 1).swapaxes(2, 3).reshape(B, S, H, Dv)


def gated_deltanet_mixer(q_raw, k_raw, v_raw, z, b_raw, a_raw, conv_w, a_log, dt_bias, o_gain):
    B, S, _ = q_raw.shape
    qkv = jnp.concatenate([q_raw, k_raw, v_raw], axis=-1)
    qkv = jax.nn.silu(causal_depthwise_conv(qkv, conv_w).astype(jnp.float32))
    q, k, v = jnp.split(qkv, 3, axis=-1)
    q = l2_normalize(q.reshape(B, S, GDN_HEADS, GDN_HEAD_DIM)) * (GDN_HEAD_DIM ** -0.5)
    k = l2_normalize(k.reshape(B, S, GDN_HEADS, GDN_HEAD_DIM))
    v = v.reshape(B, S, GDN_HEADS, GDN_HEAD_DIM)
    beta = jax.nn.sigmoid(b_raw.astype(jnp.float32))
    g = -jnp.exp(a_log.astype(jnp.float32)) * jax.nn.softplus(a_raw.astype(jnp.float32) + dt_bias.astype(jnp.float32))
    o = gated_delta_rule_chunked(q, k, v, g, beta)
    gate = jax.nn.silu(z.astype(jnp.float32)).reshape(B, S, GDN_HEADS, GDN_HEAD_DIM)
    o = rms_norm(o, o_gain) * gate
    return o.reshape(B, S, GDN_WIDTH).astype(q_raw.dtype)


def peer_ffn(h, w_query, sub_keys_1, sub_keys_2, expert_u, expert_v):
    B, S, D = h.shape
    hb = h.reshape(-1, PEER_TOKEN_BLOCK, D)

    def block(h_t):
        T = h_t.shape[0]
        q = jnp.einsum('td,dq->tq', h_t, w_query).reshape(T, PEER_HEADS, 2, PEER_HALF)
        s1 = jnp.einsum('thc,hkc->thk', q[:, :, 0], sub_keys_1)
        s2 = jnp.einsum('thc,hkc->thk', q[:, :, 1], sub_keys_2)
        v1, i1 = lax.top_k(s1, PEER_TOPK)
        v2, i2 = lax.top_k(s2, PEER_TOPK)
        cand_s = (v1[..., :, None] + v2[..., None, :]).reshape(T, PEER_HEADS, PEER_TOPK * PEER_TOPK)
        cand_i = (i1[..., :, None] * PEER_N_KEYS + i2[..., None, :]).reshape(T, PEER_HEADS, PEER_TOPK * PEER_TOPK)
        top_s, top_pos = lax.top_k(cand_s, PEER_TOPK)
        experts = jnp.take_along_axis(cand_i, top_pos, axis=-1).reshape(T, PEER_HEADS * PEER_TOPK)
        gates = jax.nn.softmax(top_s.astype(jnp.float32), axis=-1).reshape(T, PEER_HEADS * PEER_TOPK)
        u = expert_u[experts]
        act = jax.nn.gelu(jnp.einsum('td,tkd->tk', h_t, u).astype(jnp.float32), approximate=False)
        vv = expert_v[experts]
        return jnp.einsum('tk,tkd->td', (gates * act).astype(h_t.dtype), vv).astype(h_t.dtype)

    return lax.map(block, hb).reshape(B, S, D)


def setup_inputs(seed: int = 0) -> dict:
    key = jax.random.key(seed)
    ks = jax.random.split(key, 20)
    f32 = jnp.float32
    L = DEPTH
    x = jax.random.normal(ks[0], (BATCH, SEQ, D_MODEL), f32)
    norm_mix_gain = 1.0 + 0.1 * jax.random.normal(ks[1], (L, D_MODEL), f32)
    w_in = jax.random.normal(ks[2], (L, D_MODEL, IN_PROJ_COLS), f32) * D_MODEL ** -0.5
    conv_w = jax.random.normal(ks[3], (L, CONV_WIDTH, 3 * GDN_WIDTH), f32) * CONV_WIDTH ** -0.5
    a_log = jnp.log(jax.random.uniform(ks[4], (L, GDN_HEADS), f32, 1.0, 16.0))
    dt = jnp.exp(jax.random.uniform(ks[5], (L, GDN_HEADS), f32, float(np.log(1e-3)), float(np.log(1e-1))))
    dt_bias = dt + jnp.log(-jnp.expm1(-dt))
    gdn_out_gain = 1.0 + 0.1 * jax.random.normal(ks[6], (L, GDN_HEAD_DIM), f32)
    w_pool = jax.random.normal(ks[7], (L, POOL_GROUPS, POOL_GROUP_DIM, POOL_GROUP_DIM), f32) * POOL_GROUP_DIM ** -0.5
    b_pool = 0.01 * jax.random.normal(ks[8], (L, POOL_GROUPS, POOL_GROUP_DIM), f32)
    pool_scale = 1.0 + 0.1 * jax.random.normal(ks[9], (L, POOL_WIDTH), f32)
    w_out = jax.random.normal(ks[10], (L, MIX_WIDTH, D_MODEL), f32) * MIX_WIDTH ** -0.5
    norm_ffn_gain = 1.0 + 0.1 * jax.random.normal(ks[11], (L, D_MODEL), f32)
    peer_w_query = jax.random.normal(ks[12], (L, D_MODEL, PEER_HEADS * PEER_QUERY_DIM), f32) * D_MODEL ** -0.5
    peer_sub_keys_1 = jax.random.normal(ks[13], (L, PEER_HEADS, PEER_N_KEYS, PEER_HALF), f32) * PEER_HALF ** -0.5
    peer_sub_keys_2 = jax.random.normal(ks[14], (L, PEER_HEADS, PEER_N_KEYS, PEER_HALF), f32) * PEER_HALF ** -0.5
    peer_expert_u = jax.random.normal(ks[15], (L, PEER_N_EXPERTS, D_MODEL), f32) * D_MODEL ** -0.5
    peer_expert_v = jax.random.normal(ks[16], (L, PEER_N_EXPERTS, D_MODEL), f32) * 0.5
    norm_final_gain = 1.0 + 0.1 * jax.random.normal(ks[17], (D_MODEL,), f32)
    return {'x': x, 'norm_mix_gain': norm_mix_gain, 'w_in': w_in, 'conv_w': conv_w,
            'a_log': a_log, 'dt_bias': dt_bias, 'gdn_out_gain': gdn_out_gain,
            'w_pool': w_pool, 'b_pool': b_pool, 'pool_scale': pool_scale, 'w_out': w_out,
            'norm_ffn_gain': norm_ffn_gain, 'peer_w_query': peer_w_query,
            'peer_sub_keys_1': peer_sub_keys_1, 'peer_sub_keys_2': peer_sub_keys_2,
            'peer_expert_u': peer_expert_u, 'peer_expert_v': peer_expert_v,
            'norm_final_gain': norm_final_gain}


def reference(x, norm_mix_gain, w_in, conv_w, a_log, dt_bias, gdn_out_gain, w_pool, b_pool,
              pool_scale, w_out, norm_ffn_gain, peer_w_query, peer_sub_keys_1, peer_sub_keys_2,
              peer_expert_u, peer_expert_v, norm_final_gain):
    h = x
    for layer in range(DEPTH):
        hn = rms_norm(h, norm_mix_gain[layer])
        proj = jnp.einsum('bsd,dc->bsc', hn, w_in[layer])
        p_in, q_raw, k_raw, v_raw, z, b_raw, a_raw = jnp.split(proj, SPLIT_POINTS, axis=-1)
        pool_out = multiscale_pool_mixer(p_in, w_pool[layer], b_pool[layer], pool_scale[layer])
        gdn_out = gated_deltanet_mixer(q_raw, k_raw, v_raw, z, b_raw, a_raw, conv_w[layer],
                                       a_log[layer], dt_bias[layer], gdn_out_gain[layer])
        mixed = jnp.concatenate([pool_out, gdn_out], axis=-1)
        h = h + jnp.einsum('bsc,cd->bsd', mixed, w_out[layer]).astype(h.dtype)
        hf = rms_norm(h, norm_ffn_gain[layer])
        h = h + peer_ffn(hf, peer_w_query[layer], peer_sub_keys_1[layer], peer_sub_keys_2[layer],
                         peer_expert_u[layer], peer_expert_v[layer])
    return rms_norm(h, norm_final_gain)
```

```python
import functools
import math

import jax
import jax.numpy as jnp
import numpy as np
from jax import lax
from jax.experimental import pallas as pl
from jax.experimental.pallas import tpu as pltpu

NORM_EPS = 1e-6
CHUNK = 64
POOL_WINDOWS = (2, 4, 8, 16)
CONV_WIDTH = 4
GDN_HEADS = 4
GDN_HEAD_DIM = 128
PEER_HEADS = 8
PEER_N_KEYS = 128
PEER_HALF = 128
PEER_TOPK = 16

LANES = 128
POOL_CARRY = 16
CONV_CARRY = 8
VMEM_LIMIT = 56 * 1024 * 1024

HIGHEST = lax.Precision.HIGHEST


def _dot(a, b, precision=None):
    return jnp.dot(a, b, preferred_element_type=jnp.float32, precision=precision)


def _dot_nt(a, b, precision=None):
    return lax.dot_general(a, b, (((1,), (1,)), ((), ())),
                           preferred_element_type=jnp.float32, precision=precision)


def _dot_tn(a, b, precision=None):
    return lax.dot_general(a, b, (((0,), (0,)), ((), ())),
                           preferred_element_type=jnp.float32, precision=precision)


def _rms_rows(x, gain):
    return x * lax.rsqrt(jnp.mean(x * x, axis=-1, keepdims=True) + NORM_EPS) * gain


def _mix_in_kernel(x_ref, gain_ref, wp_ref, wqkv_ref, wz_ref, wba_ref, convw_ref, alog_ref, dtb_ref,
                   wpool_ref, bpool_ref, pscale_ref,
                   pool_ref, q_ref, k_ref, v_ref, gate_ref, bg_ref,
                   pbuf, cbuf, *, ts):
    s = pl.program_id(1)

    @pl.when(s == 0)
    def _():
        pbuf[0:POOL_CARRY, :] = jnp.zeros((POOL_CARRY, pbuf.shape[1]), jnp.float32)
        cbuf[0:CONV_CARRY, :] = jnp.zeros((CONV_CARRY, cbuf.shape[1]), jnp.float32)

    @pl.when(s > 0)
    def _():
        pbuf[0:POOL_CARRY, :] = pbuf[ts:ts + POOL_CARRY, :]
        cbuf[0:CONV_CARRY, :] = cbuf[ts:ts + CONV_CARRY, :]

    x = x_ref[0]
    hn = _rms_rows(x, gain_ref[...]).astype(jnp.bfloat16)

    p = _dot(hn, wp_ref[...])
    pbuf[POOL_CARRY:POOL_CARRY + ts, :] = p
    pos1 = (lax.broadcasted_iota(jnp.int32, (ts, LANES), 0) + s * ts + 1).astype(jnp.float32)
    for g, win in enumerate(POOL_WINDOWS):
        cols = slice(g * LANES, (g + 1) * LANES)
        ws = pbuf[POOL_CARRY:POOL_CARRY + ts, cols]
        for j in range(1, win):
            ws = ws + pbuf[POOL_CARRY - j:POOL_CARRY - j + ts, cols]
        d = ws / jnp.minimum(pos1, float(win)) - p[:, cols]
        y = _dot(d.astype(jnp.bfloat16), wpool_ref[g]) + bpool_ref[:, cols]
        pool_ref[0, :, cols] = y * pscale_ref[:, cols]

    cbuf[CONV_CARRY:CONV_CARRY + ts, :] = _dot(hn, wqkv_ref[...])
    acc = convw_ref[0:1, :] * cbuf[CONV_CARRY - 3:CONV_CARRY - 3 + ts, :]
    for j in range(1, CONV_WIDTH):
        acc = acc + convw_ref[j:j + 1, :] * cbuf[CONV_CARRY - 3 + j:CONV_CARRY - 3 + j + ts, :]
    qkv = acc * jax.nn.sigmoid(acc)
    width = GDN_HEADS * GDN_HEAD_DIM
    for h in range(GDN_HEADS):
        cq = slice(h * GDN_HEAD_DIM, (h + 1) * GDN_HEAD_DIM)
        ck = slice(width + h * GDN_HEAD_DIM, width + (h + 1) * GDN_HEAD_DIM)
        cv = slice(2 * width + h * GDN_HEAD_DIM, 2 * width + (h + 1) * GDN_HEAD_DIM)
        qh = qkv[:, cq]
        kh = qkv[:, ck]
        q_ref[0, :, cq] = qh * lax.rsqrt(jnp.sum(qh * qh, axis=-1, keepdims=True) + NORM_EPS) * (GDN_HEAD_DIM ** -0.5)
        k_ref[0, :, cq] = kh * lax.rsqrt(jnp.sum(kh * kh, axis=-1, keepdims=True) + NORM_EPS)
        v_ref[0, :, cq] = qkv[:, cv]

    z = _dot(hn, wz_ref[...])
    gate_ref[0] = z * jax.nn.sigmoid(z)

    ba = _dot(hn, wba_ref[...])
    lane = lax.broadcasted_iota(jnp.int32, ba.shape, 1)
    sp_in = ba + dtb_ref[...]
    softplus = jnp.maximum(sp_in, 0.0) + jnp.log1p(jnp.exp(-jnp.abs(sp_in)))
    gdec = -jnp.exp(alog_ref[...]) * softplus
    bg_ref[0] = jnp.where(lane < GDN_HEADS, jax.nn.sigmoid(ba),
                          jnp.where(lane < 2 * GDN_HEADS, gdec, 0.0))


def _mix_in(x, gain, wp, wqkv, wz, wba, convw, alog_pad, dtb_pad, wpool, bpool, pscale, *, ts):
    B, S, D = x.shape
    pw = wp.shape[1]
    gw = wz.shape[1]
    full = lambda shape: pl.BlockSpec(shape, lambda b, s: (0,) * len(shape))
    tile = lambda w: pl.BlockSpec((1, ts, w), lambda b, s: (b, s, 0))
    out_shapes = (
        jax.ShapeDtypeStruct((B, S, pw), jnp.float32),
        jax.ShapeDtypeStruct((B, S, gw), jnp.float32),
        jax.ShapeDtypeStruct((B, S, gw), jnp.float32),
        jax.ShapeDtypeStruct((B, S, gw), jnp.float32),
        jax.ShapeDtypeStruct((B, S, gw), jnp.float32),
        jax.ShapeDtypeStruct((B, S, LANES), jnp.float32),
    )
    return pl.pallas_call(
        functools.partial(_mix_in_kernel, ts=ts),
        grid=(B, S // ts),
        in_specs=[tile(D), full(gain.shape), full(wp.shape), full(wqkv.shape), full(wz.shape),
                  full(wba.shape), full(convw.shape), full(alog_pad.shape), full(dtb_pad.shape),
                  full(wpool.shape), full(bpool.shape), full(pscale.shape)],
        out_specs=(tile(pw), tile(gw), tile(gw), tile(gw), tile(gw), tile(LANES)),
        out_shape=out_shapes,
        scratch_shapes=[pltpu.VMEM((POOL_CARRY + ts, pw), jnp.float32),
                        pltpu.VMEM((CONV_CARRY + ts, wqkv.shape[1]), jnp.float32)],
        compiler_params=pltpu.CompilerParams(
            dimension_semantics=("arbitrary", "arbitrary"), vmem_limit_bytes=VMEM_LIMIT),
        name="mix_in",
    )(x, gain, wp, wqkv, wz, wba, convw, alog_pad, dtb_pad, wpool, bpool, pscale)


def _gdn_out_kernel(x_ref, pool_ref, q_ref, k_ref, v_ref, gate_ref, bg_ref, ogain_ref, wout_ref,
                    ltri_ones_ref, sel_ref,
                    h_ref, state, o_buf, *, ts):
    s = pl.program_id(1)
    H, Dh, C = GDN_HEADS, GDN_HEAD_DIM, CHUNK
    R = H * C

    @pl.when(s == 0)
    def _():
        state[...] = jnp.zeros(state.shape, jnp.float32)

    ri = lax.broadcasted_iota(jnp.int32, (R, R), 0)
    ci = lax.broadcasted_iota(jnp.int32, (R, R), 1)
    same = (ri // C) == (ci // C)
    causal = same & (ri >= ci)
    strict = same & (ri > ci)
    eye = (ri == ci).astype(jnp.float32)
    lane = lax.broadcasted_iota(jnp.int32, (C, LANES), 1)
    ltri = ltri_ones_ref[0:R, :]

    for c in range(ts // C):
        rows = slice(c * C, (c + 1) * C)
        bg = bg_ref[0, rows, :]
        bgm = jnp.concatenate(
            [jnp.where((lane == h) | (lane == H + h), bg, 0.0) for h in range(H)], axis=0)
        bgb = _dot(bgm, sel_ref[...], HIGHEST)
        g_b = bgb[:, 0:R]
        beta_b = bgb[:, R:R + Dh]
        ct = _dot(ltri_ones_ref[...], g_b[:, 0:Dh], HIGHEST)
        gcum = ct[0:R]
        gtot = ct[R:2 * R]
        diff = _dot(ltri, jnp.where(strict, g_b, 0.0), HIGHEST)
        dec = jnp.where(causal, jnp.exp(jnp.where(causal, diff, 0.0)), 0.0)

        qs = jnp.concatenate([q_ref[0, rows, h * Dh:(h + 1) * Dh] for h in range(H)], axis=0)
        ks = jnp.concatenate([k_ref[0, rows, h * Dh:(h + 1) * Dh] for h in range(H)], axis=0)
        vs = jnp.concatenate([v_ref[0, rows, h * Dh:(h + 1) * Dh] for h in range(H)], axis=0)
        kb = ks * beta_b
        vb = vs * beta_b
        kk = _dot_nt(kb, ks)
        m = jnp.where(strict, -(kk * dec), 0.0)
        tinv = eye + m
        pw = m
        for _ in range(int(math.log2(C)) - 1):
            pw = _dot(pw, pw, HIGHEST)
            tinv = tinv + _dot(pw, tinv, HIGHEST)
        egc = jnp.exp(gcum)
        sol = _dot(tinv, jnp.concatenate([vb, kb * egc], axis=1), HIGHEST)
        u = sol[:, 0:Dh]
        w = sol[:, Dh:2 * Dh]
        qk = _dot_nt(qs, ks) * dec
        qd = qs * egc
        kd = ks * jnp.exp(gtot - gcum)
        cd = jnp.exp(gtot)

        vnew = jnp.concatenate(
            [u[h * C:(h + 1) * C] - _dot(w[h * C:(h + 1) * C], state[h]) for h in range(H)], axis=0)
        o_intra = _dot(qk, vnew)
        for h in range(H):
            hr = slice(h * C, (h + 1) * C)
            o_buf[rows, h * Dh:(h + 1) * Dh] = _dot(qd[hr], state[h]) + o_intra[hr]
            state[h] = state[h] * cd[h * C:h * C + 1, :] + _dot_tn(kd[hr], vnew[hr])

    o = o_buf[...]
    gdn = jnp.concatenate(
        [o[:, h * Dh:(h + 1) * Dh]
         * lax.rsqrt(jnp.mean(o[:, h * Dh:(h + 1) * Dh] * o[:, h * Dh:(h + 1) * Dh], axis=-1, keepdims=True) + NORM_EPS)
         for h in range(H)], axis=1)
    gdn = gdn * ogain_ref[...] * gate_ref[0]
    mixed = jnp.concatenate([pool_ref[0], gdn], axis=1).astype(jnp.bfloat16)
    h_ref[0] = x_ref[0] + _dot(mixed, wout_ref[...])


def _gdn_out(x, pool, q, k, v, gate, bg, ogain, wout, ltri_ones, sel, *, ts):
    B, S, D = x.shape
    gw = q.shape[2]
    full = lambda shape: pl.BlockSpec(shape, lambda b, s: (0,) * len(shape))
    tile = lambda w: pl.BlockSpec((1, ts, w), lambda b, s: (b, s, 0))
    return pl.pallas_call(
        functools.partial(_gdn_out_kernel, ts=ts),
        grid=(B, S // ts),
        in_specs=[tile(D), tile(pool.shape[2]), tile(gw), tile(gw), tile(gw), tile(gw), tile(LANES),
                  full(ogain.shape), full(wout.shape), full(ltri_ones.shape), full(sel.shape)],
        out_specs=tile(D),
        out_shape=jax.ShapeDtypeStruct((B, S, D), jnp.float32),
        scratch_shapes=[pltpu.VMEM((GDN_HEADS, GDN_HEAD_DIM, GDN_HEAD_DIM), jnp.float32),
                        pltpu.VMEM((ts, gw), jnp.float32)],
        compiler_params=pltpu.CompilerParams(
            dimension_semantics=("arbitrary", "arbitrary"), vmem_limit_bytes=VMEM_LIMIT),
        name="gdn_out",
    )(x, pool, q, k, v, gate, bg, ogain, wout, ltri_ones, sel)


def _gdn_constants():
    H, C, Dh = GDN_HEADS, CHUNK, GDN_HEAD_DIM
    R = H * C
    idx = np.arange(R)
    same = (idx[:, None] // C) == (idx[None, :] // C)
    ltri = (same & (idx[:, None] >= idx[None, :])).astype(np.float32)
    ones = same.astype(np.float32)
    sel = np.zeros((LANES, R + Dh), np.float32)
    sel[H:2 * H, 0:R] = 1.0
    sel[0:H, R:R + Dh] = 1.0
    return jnp.asarray(np.concatenate([ltri, ones], axis=0)), jnp.asarray(sel)


def _top_values(x, n):
    tops = []
    for i in range(n):
        m = jnp.max(x, axis=0, keepdims=True)
        tops.append(m)
        if i + 1 < n:
            x = jnp.where(x == m, -jnp.inf, x)
    return tops


def _peer_prep_kernel(h_ref, gain_ref, wqt_ref, k1_ref, k2_ref,
                      hft_ref, theta_ref, e1_ref, s2_ref, e2_ref):
    NH, K = PEER_HEADS, PEER_TOPK
    hf = _rms_rows(h_ref[...], gain_ref[...])
    hft = hf.T.astype(jnp.bfloat16)
    hft_ref[...] = hft
    qt = _dot(wqt_ref[...], hft)

    s1, s2, a_top, b_top = [], [], [], []
    for h in range(NH):
        base = h * 2 * PEER_HALF
        s1h = _dot(k1_ref[h], qt[base:base + PEER_HALF])
        s2h = _dot(k2_ref[h], qt[base + PEER_HALF:base + 2 * PEER_HALF])
        s1.append(s1h)
        s2.append(s2h)
        a_top.append(_top_values(s1h, K + 1))
        b_top.append(_top_values(s2h, K + 1))

    A = [jnp.concatenate([a_top[h][i] for h in range(NH)], axis=0) for i in range(K + 1)]
    Bv = [jnp.concatenate([b_top[h][j] for h in range(NH)], axis=0) for j in range(K + 1)]
    cells = [(i, j) for i in range(K + 1) for j in range(K + 1) if (i + 1) * (j + 1) <= K + 1]
    cand = [A[i] + Bv[j] for (i, j) in cells]
    work = list(cand)
    kth = None
    for r in range(K + 1):
        m = work[0]
        for cnd in work[1:]:
            m = jnp.maximum(m, cnd)
        if r == K - 1:
            kth = m
        if r < K:
            work = [jnp.where(cnd == m, -jnp.inf, cnd) for cnd in work]
    nxt = m
    mid = 0.5 * (kth + nxt)
    tau = jnp.where(mid > nxt, mid, kth)
    top = A[0] + Bv[0]
    z = jnp.zeros_like(tau)
    for cnd in cand:
        z = z + jnp.where(cnd >= tau, jnp.exp(cnd - top), 0.0)
    zinv = 1.0 / z

    for h in range(NH):
        theta_ref[h] = tau[h:h + 1, :] - s1[h]
        e1_ref[h] = jnp.exp(s1[h] - A[0][h:h + 1, :]) * zinv[h:h + 1, :]
        s2_ref[h] = s2[h]
        e2_ref[h] = jnp.exp(s2[h] - Bv[0][h:h + 1, :])


def _peer_prep(h1, gain, wqt, k1, k2, *, tt):
    N, D = h1.shape
    NH, NK = k1.shape[0], k1.shape[1]
    full = lambda shape: pl.BlockSpec(shape, lambda i: (0,) * len(shape))
    head_out = pl.BlockSpec((NH, NK, tt), lambda i: (0, 0, i))
    head_shape = jax.ShapeDtypeStruct((NH, NK, N), jnp.float32)
    return pl.pallas_call(
        _peer_prep_kernel,
        grid=(N // tt,),
        in_specs=[pl.BlockSpec((tt, D), lambda i: (i, 0)), full(gain.shape), full(wqt.shape),
                  full(k1.shape), full(k2.shape)],
        out_specs=(pl.BlockSpec((D, tt), lambda i: (0, i)), head_out, head_out, head_out, head_out),
        out_shape=(jax.ShapeDtypeStruct((D, N), jnp.bfloat16), head_shape, head_shape, head_shape, head_shape),
        compiler_params=pltpu.CompilerParams(
            dimension_semantics=("arbitrary",), vmem_limit_bytes=VMEM_LIMIT),
        name="peer_prep",
    )(h1, gain, wqt, k1, k2)


def _peer_ffn_kernel(u_ref, vt_ref, hft_ref, theta_ref, e1_ref, s2_ref, e2_ref, h_ref, gain_ref,
                     out_ref, acc, w_buf, *, eb):
    j = pl.program_id(1)
    NH, NK = PEER_HEADS, PEER_N_KEYS

    @pl.when(j == 0)
    def _():
        acc[...] = jnp.zeros(acc.shape, jnp.float32)

    act = _dot(u_ref[...], hft_ref[...])
    for a in range(eb // NK):
        rows = slice(a * NK, (a + 1) * NK)
        g = None
        for h in range(NH):
            hit = s2_ref[h] >= theta_ref[h, a:a + 1, :]
            term = jnp.where(hit, e2_ref[h] * e1_ref[h, a:a + 1, :], 0.0)
            g = term if g is None else g + term
        xa = act[rows]
        gelu = 0.5 * xa * (1.0 + lax.erf(xa * np.float32(math.sqrt(0.5))))
        w_buf[rows, :] = (g * gelu).astype(jnp.bfloat16)
    acc[...] += _dot(vt_ref[...], w_buf[...])

    @pl.when(j == pl.num_programs(1) - 1)
    def _():
        h2 = h_ref[...] + acc[...].T
        out_ref[...] = _rms_rows(h2, gain_ref[...])


def _peer_ffn(u_bf, vt_bf, hft, theta, e1, s2, e2, h1, gain, *, tt, eb):
    N, D = h1.shape
    E = u_bf.shape[0]
    NH, NK = PEER_HEADS, PEER_N_KEYS
    ab = eb // NK
    return pl.pallas_call(
        functools.partial(_peer_ffn_kernel, eb=eb),
        grid=(N // tt, E // eb),
        in_specs=[pl.BlockSpec((eb, D), lambda i, j: (j, 0)),
                  pl.BlockSpec((D, eb), lambda i, j: (0, j)),
                  pl.BlockSpec((D, tt), lambda i, j: (0, i)),
                  pl.BlockSpec((NH, ab, tt), lambda i, j: (0, j, i)),
                  pl.BlockSpec((NH, ab, tt), lambda i, j: (0, j, i)),
                  pl.BlockSpec((NH, NK, tt), lambda i, j: (0, 0, i)),
                  pl.BlockSpec((NH, NK, tt), lambda i, j: (0, 0, i)),
                  pl.BlockSpec((tt, D), lambda i, j: (i, 0)),
                  pl.BlockSpec(gain.shape, lambda i, j: (0, 0))],
        out_specs=pl.BlockSpec((tt, D), lambda i, j: (i, 0)),
        out_shape=jax.ShapeDtypeStruct((N, D), jnp.float32),
        scratch_shapes=[pltpu.VMEM((D, tt), jnp.float32), pltpu.VMEM((eb, tt), jnp.bfloat16)],
        compiler_params=pltpu.CompilerParams(
            dimension_semantics=("arbitrary", "arbitrary"), vmem_limit_bytes=VMEM_LIMIT),
        name="peer_ffn",
    )(u_bf, vt_bf, hft, theta, e1, s2, e2, h1, gain)


def _tiles(B, S):
    ts_in = min(S, 512)
    ts_gdn = min(S, 256)
    tt = min(B * S, 512)
    eb = 1024
    return ts_in, ts_gdn, tt, eb


def kernel(x, norm_mix_gain, w_in, conv_w, a_log, dt_bias, gdn_out_gain, w_pool, b_pool, pool_scale,
           w_out, norm_ffn_gain, peer_w_query, peer_sub_keys_1, peer_sub_keys_2, peer_expert_u,
           peer_expert_v, norm_final_gain):
    B, S, D = x.shape
    depth = w_in.shape[0]
    pw = pool_scale.shape[1]
    gw = GDN_HEADS * GDN_HEAD_DIM
    H = GDN_HEADS
    ts_in, ts_gdn, tt, eb = _tiles(B, S)
    assert S % ts_in == 0 and S % ts_gdn == 0 and ts_gdn % CHUNK == 0 and (B * S) % tt == 0
    bf = jnp.bfloat16
    ltri_ones, sel = _gdn_constants()

    h = x
    for l in range(depth):
        wl = w_in[l]
        wp = wl[:, 0:pw].astype(bf)
        wqkv = wl[:, pw:pw + 3 * gw].astype(bf)
        wz = wl[:, pw + 3 * gw:pw + 4 * gw].astype(bf)
        wba = jnp.pad(wl[:, pw + 4 * gw:pw + 4 * gw + 2 * H], ((0, 0), (0, LANES - 2 * H))).astype(bf)
        alog_pad = jnp.pad(a_log[l], (H, LANES - 2 * H))[None, :]
        dtb_pad = jnp.pad(dt_bias[l], (H, LANES - 2 * H))[None, :]
        pool, q, k, v, gate, bg = _mix_in(
            h, norm_mix_gain[l][None, :], wp, wqkv, wz, wba, conv_w[l], alog_pad, dtb_pad,
            w_pool[l].astype(bf), b_pool[l].reshape(1, pw), pool_scale[l][None, :], ts=ts_in)
        h1 = _gdn_out(h, pool, q, k, v, gate, bg, jnp.tile(gdn_out_gain[l], H)[None, :],
                      w_out[l].astype(bf), ltri_ones, sel, ts=ts_gdn)

        h1f = h1.reshape(B * S, D)
        hft, theta, e1, s2, e2 = _peer_prep(
            h1f, norm_ffn_gain[l][None, :], peer_w_query[l].T.astype(bf),
            peer_sub_keys_1[l], peer_sub_keys_2[l], tt=tt)
        assert depth == 1
        out = _peer_ffn(peer_expert_u[l].astype(bf), peer_expert_v[l].T.astype(bf), hft, theta, e1, s2, e2,
                        h1f, norm_final_gain[None, :], tt=tt, eb=eb)
        h = out.reshape(B, S, D)
    return h
```

```python
import functools
import math

import jax
import jax.numpy as jnp
import numpy as np
from jax import lax
from jax.experimental import pallas as pl
from jax.experimental.pallas import tpu as pltpu

NORM_EPS = 1e-6
CHUNK = 64
POOL_WINDOWS = (2, 4, 8, 16)
CONV_WIDTH = 4
GDN_HEADS = 4
GDN_HEAD_DIM = 128
PEER_HEADS = 8
PEER_N_KEYS = 128
PEER_HALF = 128
PEER_TOPK = 16
PEER_SUB = 256

LANES = 128
POOL_CARRY = 16
CONV_CARRY = 8
VMEM_LIMIT = 56 * 1024 * 1024


def _dot(a, b):
    return jnp.dot(a, b, preferred_element_type=jnp.float32)


def _dot_nt(a, b):
    return lax.dot_general(a, b, (((1,), (1,)), ((), ())), preferred_element_type=jnp.float32)


def _dot_tn(a, b):
    return lax.dot_general(a, b, (((0,), (0,)), ((), ())), preferred_element_type=jnp.float32)


def _rms_rows(x, gain):
    return x * lax.rsqrt(jnp.mean(x * x, axis=-1, keepdims=True) + NORM_EPS) * gain


def _mix_in_kernel(x_ref, gain_ref, wp_ref, wqkv_ref, wz_ref, wba_ref, convw_ref, alog_ref, dtb_ref,
                   wpool_ref, bpool_ref, pscale_ref,
                   pool_ref, q_ref, k_ref, v_ref, gate_ref, bg_ref,
                   pbuf, cbuf, *, ts):
    s = pl.program_id(1)

    @pl.when(s == 0)
    def _():
        pbuf[0:POOL_CARRY, :] = jnp.zeros((POOL_CARRY, pbuf.shape[1]), jnp.float32)
        cbuf[0:CONV_CARRY, :] = jnp.zeros((CONV_CARRY, cbuf.shape[1]), jnp.float32)

    @pl.when(s > 0)
    def _():
        pbuf[0:POOL_CARRY, :] = pbuf[ts:ts + POOL_CARRY, :]
        cbuf[0:CONV_CARRY, :] = cbuf[ts:ts + CONV_CARRY, :]

    x = x_ref[0]
    hn = _rms_rows(x, gain_ref[...]).astype(jnp.bfloat16)

    p = _dot(hn, wp_ref[...])
    pbuf[POOL_CARRY:POOL_CARRY + ts, :] = p
    pos1 = (lax.broadcasted_iota(jnp.int32, (ts, LANES), 0) + s * ts + 1).astype(jnp.float32)
    for g, win in enumerate(POOL_WINDOWS):
        cols = slice(g * LANES, (g + 1) * LANES)
        ws = pbuf[POOL_CARRY:POOL_CARRY + ts, cols]
        for j in range(1, win):
            ws = ws + pbuf[POOL_CARRY - j:POOL_CARRY - j + ts, cols]
        d = ws / jnp.minimum(pos1, float(win)) - p[:, cols]
        y = _dot(d.astype(jnp.bfloat16), wpool_ref[g]) + bpool_ref[:, cols]
        pool_ref[0, :, cols] = y * pscale_ref[:, cols]

    cbuf[CONV_CARRY:CONV_CARRY + ts, :] = _dot(hn, wqkv_ref[...])
    acc = convw_ref[0:1, :] * cbuf[CONV_CARRY - 3:CONV_CARRY - 3 + ts, :]
    for j in range(1, CONV_WIDTH):
        acc = acc + convw_ref[j:j + 1, :] * cbuf[CONV_CARRY - 3 + j:CONV_CARRY - 3 + j + ts, :]
    qkv = acc * jax.nn.sigmoid(acc)
    width = GDN_HEADS * GDN_HEAD_DIM
    for h in range(GDN_HEADS):
        cq = slice(h * GDN_HEAD_DIM, (h + 1) * GDN_HEAD_DIM)
        ck = slice(width + h * GDN_HEAD_DIM, width + (h + 1) * GDN_HEAD_DIM)
        cv = slice(2 * width + h * GDN_HEAD_DIM, 2 * width + (h + 1) * GDN_HEAD_DIM)
        qh = qkv[:, cq]
        kh = qkv[:, ck]
        q_ref[0, :, cq] = qh * lax.rsqrt(jnp.sum(qh * qh, axis=-1, keepdims=True) + NORM_EPS) * (GDN_HEAD_DIM ** -0.5)
        k_ref[0, :, cq] = kh * lax.rsqrt(jnp.sum(kh * kh, axis=-1, keepdims=True) + NORM_EPS)
        v_ref[0, :, cq] = qkv[:, cv]

    z = _dot(hn, wz_ref[...])
    gate_ref[0] = z * jax.nn.sigmoid(z)

    ba = _dot(hn, wba_ref[...])
    lane = lax.broadcasted_iota(jnp.int32, ba.shape, 1)
    sp_in = ba + dtb_ref[...]
    softplus = jnp.maximum(sp_in, 0.0) + jnp.log1p(jnp.exp(-jnp.abs(sp_in)))
    gdec = -jnp.exp(alog_ref[...]) * softplus
    bg_ref[0] = jnp.where(lane < GDN_HEADS, jax.nn.sigmoid(ba),
                          jnp.where(lane < 2 * GDN_HEADS, gdec, 0.0))


def _mix_in(x, gain, wp, wqkv, wz, wba, convw, alog_pad, dtb_pad, wpool, bpool, pscale, *, ts):
    B, S, D = x.shape
    pw = wp.shape[1]
    gw = wz.shape[1]
    full = lambda shape: pl.BlockSpec(shape, lambda b, s: (0,) * len(shape))
    tile = lambda w: pl.BlockSpec((1, ts, w), lambda b, s: (b, s, 0))
    out_shapes = (
        jax.ShapeDtypeStruct((B, S, pw), jnp.float32),
        jax.ShapeDtypeStruct((B, S, gw), jnp.float32),
        jax.ShapeDtypeStruct((B, S, gw), jnp.float32),
        jax.ShapeDtypeStruct((B, S, gw), jnp.float32),
        jax.ShapeDtypeStruct((B, S, gw), jnp.float32),
        jax.ShapeDtypeStruct((B, S, LANES), jnp.float32),
    )
    return pl.pallas_call(
        functools.partial(_mix_in_kernel, ts=ts),
        grid=(B, S // ts),
        in_specs=[tile(D), full(gain.shape), full(wp.shape), full(wqkv.shape), full(wz.shape),
                  full(wba.shape), full(convw.shape), full(alog_pad.shape), full(dtb_pad.shape),
                  full(wpool.shape), full(bpool.shape), full(pscale.shape)],
        out_specs=(tile(pw), tile(gw), tile(gw), tile(gw), tile(gw), tile(LANES)),
        out_shape=out_shapes,
        scratch_shapes=[pltpu.VMEM((POOL_CARRY + ts, pw), jnp.float32),
                        pltpu.VMEM((CONV_CARRY + ts, wqkv.shape[1]), jnp.float32)],
        compiler_params=pltpu.CompilerParams(
            dimension_semantics=("arbitrary", "arbitrary"), vmem_limit_bytes=VMEM_LIMIT),
        name="mix_in",
    )(x, gain, wp, wqkv, wz, wba, convw, alog_pad, dtb_pad, wpool, bpool, pscale)


def _split3(a):
    p1 = a.astype(jnp.bfloat16)
    r1 = a - p1.astype(jnp.float32)
    p2 = r1.astype(jnp.bfloat16)
    p3 = (r1 - p2.astype(jnp.float32)).astype(jnp.bfloat16)
    return p1, p2, p3


def _gdn_out_kernel(x_ref, pool_ref, q_ref, k_ref, v_ref, gate_ref, bg_ref, ogain_ref, wout_ref,
                    sel_ref, selr_ref,
                    h_ref, state, o_buf, *, ts):
    s = pl.program_id(1)
    H, Dh, C = GDN_HEADS, GDN_HEAD_DIM, CHUNK
    R = H * C
    bf = jnp.bfloat16

    @pl.when(s == 0)
    def _():
        state[...] = jnp.zeros(state.shape, jnp.float32)

    ri = lax.broadcasted_iota(jnp.int32, (R, R), 0)
    ci = lax.broadcasted_iota(jnp.int32, (R, R), 1)
    same = (ri // C) == (ci // C)
    causal = same & (ri >= ci)
    strict = same & (ri > ci)
    eye = (ri == ci).astype(jnp.float32)

    bg = bg_ref[0]
    trow = lax.broadcasted_iota(jnp.int32, (ts, LANES), 0) % C
    tlane = lax.broadcasted_iota(jnp.int32, (ts, LANES), 1)
    cum = bg
    step = 1
    while step < C:
        cum = cum + jnp.where(trow >= step, pltpu.roll(cum, step, axis=0), 0.0)
        step *= 2
    tot = jnp.concatenate(
        [jnp.broadcast_to(cum[c * C + C - 1:c * C + C, :], (C, LANES)) for c in range(ts // C)], axis=0)
    feat = jnp.where(tlane < H, bg,
                     jnp.where(tlane < 2 * H, cum,
                               jnp.where(tlane < 3 * H, pltpu.roll(tot, H, axis=1), 0.0)))
    lane = lax.broadcasted_iota(jnp.int32, (C, LANES), 1)

    for c in range(ts // C):
        rows = slice(c * C, (c + 1) * C)
        fc = feat[rows]
        fm = jnp.concatenate([jnp.where(lane % H == h, fc, 0.0) for h in range(H)], axis=0)
        fcat = jnp.concatenate(_split3(fm), axis=1)
        bc = _dot(fcat, sel_ref[...])
        gcum_rr = bc[:, 0:R]
        gcum = bc[:, 0:Dh]
        beta_b = bc[:, R:R + Dh]
        gtot = bc[:, R + Dh:R + 2 * Dh]
        gcum_cols = _dot_nt(selr_ref[...], fcat)
        diff = gcum_rr - gcum_cols
        dec = jnp.where(causal, jnp.exp(jnp.where(causal, diff, 0.0)), 0.0)

        qs = jnp.concatenate([q_ref[0, rows, h * Dh:(h + 1) * Dh] for h in range(H)], axis=0)
        ks = jnp.concatenate([k_ref[0, rows, h * Dh:(h + 1) * Dh] for h in range(H)], axis=0)
        vs = jnp.concatenate([v_ref[0, rows, h * Dh:(h + 1) * Dh] for h in range(H)], axis=0)
        ks_bf = ks.astype(bf)
        kb = ks * beta_b
        vb = vs * beta_b
        kk = _dot_nt(kb.astype(bf), ks_bf)
        m = jnp.where(strict, -(kk * dec), 0.0)
        tinv = eye + m
        pw = m
        for _ in range(int(math.log2(C)) - 1):
            pw_bf = pw.astype(bf)
            pw = _dot(pw_bf, pw_bf)
            tinv = tinv + _dot(pw.astype(bf), tinv.astype(bf))
        egc = jnp.exp(gcum)
        sol = _dot(tinv.astype(bf), jnp.concatenate([vb, kb * egc], axis=1).astype(bf))
        u = sol[:, 0:Dh]
        w = sol[:, Dh:2 * Dh].astype(bf)
        qk = (_dot_nt(qs.astype(bf), ks_bf) * dec).astype(bf)
        qd = (qs * egc).astype(bf)
        kd = (ks * jnp.exp(gtot - gcum)).astype(bf)
        cd = jnp.exp(gtot)

        st = [state[h] for h in range(H)]
        st_bf = [t.astype(bf) for t in st]
        vnew = jnp.concatenate(
            [u[h * C:(h + 1) * C] - _dot(w[h * C:(h + 1) * C], st_bf[h]) for h in range(H)], axis=0)
        vnew_bf = vnew.astype(bf)
        o_intra = _dot(qk, vnew_bf)
        for h in range(H):
            hr = slice(h * C, (h + 1) * C)
            o_buf[rows, h * Dh:(h + 1) * Dh] = _dot(qd[hr], st_bf[h]) + o_intra[hr]
            state[h] = st[h] * cd[h * C:h * C + 1, :] + _dot_tn(kd[hr], vnew_bf[hr])

    o = o_buf[...]
    gdn = jnp.concatenate(
        [o[:, h * Dh:(h + 1) * Dh]
         * lax.rsqrt(jnp.mean(o[:, h * Dh:(h + 1) * Dh] * o[:, h * Dh:(h + 1) * Dh], axis=-1, keepdims=True) + NORM_EPS)
         for h in range(H)], axis=1)
    gdn = gdn * ogain_ref[...] * gate_ref[0]
    mixed = jnp.concatenate([pool_ref[0], gdn], axis=1).astype(jnp.bfloat16)
    h_ref[0] = x_ref[0] + _dot(mixed, wout_ref[...])


def _gdn_out(x, pool, q, k, v, gate, bg, ogain, wout, sel, selr, *, ts):
    B, S, D = x.shape
    gw = q.shape[2]
    full = lambda shape: pl.BlockSpec(shape, lambda b, s: (0,) * len(shape))
    tile = lambda w: pl.BlockSpec((1, ts, w), lambda b, s: (b, s, 0))
    return pl.pallas_call(
        functools.partial(_gdn_out_kernel, ts=ts),
        grid=(B, S // ts),
        in_specs=[tile(D), tile(pool.shape[2]), tile(gw), tile(gw), tile(gw), tile(gw), tile(LANES),
                  full(ogain.shape), full(wout.shape), full(sel.shape), full(selr.shape)],
        out_specs=tile(D),
        out_shape=jax.ShapeDtypeStruct((B, S, D), jnp.float32),
        scratch_shapes=[pltpu.VMEM((GDN_HEADS, GDN_HEAD_DIM, GDN_HEAD_DIM), jnp.float32),
                        pltpu.VMEM((ts, gw), jnp.float32)],
        compiler_params=pltpu.CompilerParams(
            dimension_semantics=("arbitrary", "arbitrary"), vmem_limit_bytes=VMEM_LIMIT),
        name="gdn_out",
    )(x, pool, q, k, v, gate, bg, ogain, wout, sel, selr)


def _gdn_constants():
    H, C, Dh = GDN_HEADS, CHUNK, GDN_HEAD_DIM
    R = H * C
    sel = np.zeros((3 * LANES, R + 2 * Dh), np.float32)
    selr = np.zeros((R, 3 * LANES), np.float32)
    for piece in range(3):
        base = piece * LANES
        sel[base + H:base + 2 * H, 0:R] = 1.0
        sel[base:base + H, R:R + Dh] = 1.0
        sel[base + 2 * H:base + 3 * H, R + Dh:R + 2 * Dh] = 1.0
        selr[:, base + H:base + 2 * H] = 1.0
    return jnp.asarray(sel, jnp.bfloat16), jnp.asarray(selr, jnp.bfloat16)


def _top_values(x, n, n_ranked=0):
    tops = []
    rank = jnp.full(x.shape, float(n_ranked), jnp.float32)
    for i in range(n):
        m = jnp.max(x, axis=0, keepdims=True)
        tops.append(m)
        if i + 1 < n:
            hit = x == m
            if i < n_ranked:
                rank = jnp.where(hit, float(i), rank)
            x = jnp.where(hit, -jnp.inf, x)
    return tops, rank


def _dup_bf16_words(x):
    hi = lax.bitcast_convert_type(x.astype(jnp.bfloat16).astype(jnp.float32), jnp.uint32)
    return hi | (hi >> 16)


def _peer_prep_kernel(h_ref, gain_ref, wqt_ref, k1_ref, k2_ref,
                      hft_ref, cnt_ref, e1_ref, r2_ref, e2_ref):
    NH, K = PEER_HEADS, PEER_TOPK
    hf = _rms_rows(h_ref[...], gain_ref[...])
    hft = hf.T.astype(jnp.bfloat16)
    hft_ref[...] = hft
    qt = _dot(wqt_ref[...], hft)

    s1, s2, a_top, b_top = [], [], [], []
    for h in range(NH):
        base = h * 2 * PEER_HALF
        s1h = _dot(k1_ref[h], qt[base:base + PEER_HALF])
        s2h = _dot(k2_ref[h], qt[base + PEER_HALF:base + 2 * PEER_HALF])
        s1.append(s1h)
        s2.append(s2h)
        a_top.append(_top_values(s1h, K + 1)[0])
        tops, rank = _top_values(s2h, K + 1, K)
        b_top.append(tops)
        r2_ref[h] = rank.astype(jnp.bfloat16)

    A = [jnp.concatenate([a_top[h][i] for h in range(NH)], axis=0) for i in range(K + 1)]
    Bv = [jnp.concatenate([b_top[h][j] for h in range(NH)], axis=0) for j in range(K + 1)]
    cells = [(i, j) for i in range(K + 1) for j in range(K + 1) if (i + 1) * (j + 1) <= K + 1]
    cand = {(i, j): A[i] + Bv[j] for (i, j) in cells}
    work = list(cand.values())
    kth = None
    for r in range(K + 1):
        m = work[0]
        for cnd in work[1:]:
            m = jnp.maximum(m, cnd)
        if r == K - 1:
            kth = m
        if r < K:
            work = [jnp.where(cnd == m, -jnp.inf, cnd) for cnd in work]
    nxt = m
    mid = 0.5 * (kth + nxt)
    tau = jnp.where(mid > nxt, mid, kth)
    top = A[0] + Bv[0]
    z = jnp.zeros_like(tau)
    counts = [jnp.zeros_like(tau) for _ in range(K)]
    for (i, j), cnd in cand.items():
        if i < K and j < K:
            hit = cnd >= tau
            z = z + jnp.where(hit, jnp.exp(cnd - top), 0.0)
            counts[i] = counts[i] + jnp.where(hit, 1.0, 0.0)
    zinv = 1.0 / z

    for h in range(NH):
        cnt = jnp.zeros_like(s1[h])
        for i in range(K):
            cnt = jnp.where(s1[h] == a_top[h][i], counts[i][h:h + 1, :], cnt)
        cnt_ref[h] = _dup_bf16_words(cnt)
        e1_ref[h] = _dup_bf16_words(jnp.exp(s1[h] - A[0][h:h + 1, :]) * zinv[h:h + 1, :])
        e2_ref[h] = jnp.exp(s2[h] - Bv[0][h:h + 1, :]).astype(jnp.bfloat16)


def _peer_prep(h1, gain, wqt, k1, k2, *, tt):
    N, D = h1.shape
    NH, NK = k1.shape[0], k1.shape[1]
    full = lambda shape: pl.BlockSpec(shape, lambda i: (0,) * len(shape))
    head_out = pl.BlockSpec((NH, NK, tt), lambda i: (0, 0, i))
    word_shape = jax.ShapeDtypeStruct((NH, NK, N), jnp.uint32)
    half_shape = jax.ShapeDtypeStruct((NH, NK, N), jnp.bfloat16)
    return pl.pallas_call(
        _peer_prep_kernel,
        grid=(N // tt,),
        in_specs=[pl.BlockSpec((tt, D), lambda i: (i, 0)), full(gain.shape), full(wqt.shape),
                  full(k1.shape), full(k2.shape)],
        out_specs=(pl.BlockSpec((D, tt), lambda i: (0, i)), head_out, head_out, head_out, head_out),
        out_shape=(jax.ShapeDtypeStruct((D, N), jnp.bfloat16), word_shape, word_shape, half_shape, half_shape),
        compiler_params=pltpu.CompilerParams(
            dimension_semantics=("arbitrary",), vmem_limit_bytes=VMEM_LIMIT),
        name="peer_prep",
    )(h1, gain, wqt, k1, k2)


def _peer_ffn_kernel(u_ref, vt_ref, hft_ref, cnt_ref, e1_ref, r2_ref, e2_ref, h_ref, gain_ref,
                     out_ref, acc, *, eb):
    j = pl.program_id(1)
    NH, NK = PEER_HEADS, PEER_N_KEYS
    tt = hft_ref.shape[1]
    bf = jnp.bfloat16

    @pl.when(j == 0)
    def _():
        acc[...] = jnp.zeros(acc.shape, jnp.float32)

    def row_bcast(ref, h, a):
        return pltpu.bitcast(jnp.broadcast_to(ref[h, a:a + 1, :], (NK // 2, tt)), bf)

    def activations(sb):
        return _dot(u_ref[sb * PEER_SUB:(sb + 1) * PEER_SUB, :], hft_ref[...])

    n_sub = eb // PEER_SUB
    act_next = activations(0)
    for sb in range(n_sub):
        r0 = sb * PEER_SUB
        act = act_next
        if sb + 1 < n_sub:
            act_next = activations(sb + 1)
        w = []
        for a2 in range(PEER_SUB // NK):
            a = sb * (PEER_SUB // NK) + a2
            g = None
            for h in range(NH):
                hit = r2_ref[h] < row_bcast(cnt_ref, h, a)
                term = jnp.where(hit, e2_ref[h] * row_bcast(e1_ref, h, a), jnp.zeros((), bf))
                g = term if g is None else g + term
            xa = act[a2 * NK:(a2 + 1) * NK]
            gelu = 0.5 * xa * (1.0 + lax.erf(xa * np.float32(math.sqrt(0.5))))
            w.append(g * gelu.astype(bf))
        acc[...] += _dot(vt_ref[:, r0:r0 + PEER_SUB], jnp.concatenate(w, axis=0))

    @pl.when(j == pl.num_programs(1) - 1)
    def _():
        h2 = h_ref[...] + acc[...].T
        out_ref[...] = _rms_rows(h2, gain_ref[...])


def _peer_ffn(u_bf, vt_bf, hft, cnt, e1, r2, e2, h1, gain, *, tt, eb):
    N, D = h1.shape
    E = u_bf.shape[0]
    NH, NK = PEER_HEADS, PEER_N_KEYS
    ab = eb // NK
    return pl.pallas_call(
        functools.partial(_peer_ffn_kernel, eb=eb),
        grid=(N // tt, E // eb),
        in_specs=[pl.BlockSpec((eb, D), lambda i, j: (j, 0)),
                  pl.BlockSpec((D, eb), lambda i, j: (0, j)),
                  pl.BlockSpec((D, tt), lambda i, j: (0, i)),
                  pl.BlockSpec((NH, ab, tt), lambda i, j: (0, j, i)),
                  pl.BlockSpec((NH, ab, tt), lambda i, j: (0, j, i)),
                  pl.BlockSpec((NH, NK, tt), lambda i, j: (0, 0, i)),
                  pl.BlockSpec((NH, NK, tt), lambda i, j: (0, 0, i)),
                  pl.BlockSpec((tt, D), lambda i, j: (i, 0)),
                  pl.BlockSpec(gain.shape, lambda i, j: (0, 0))],
        out_specs=pl.BlockSpec((tt, D), lambda i, j: (i, 0)),
        out_shape=jax.ShapeDtypeStruct((N, D), jnp.float32),
        scratch_shapes=[pltpu.VMEM((D, tt), jnp.float32)],
        compiler_params=pltpu.CompilerParams(
            dimension_semantics=("arbitrary", "arbitrary"), vmem_limit_bytes=VMEM_LIMIT),
        name="peer_ffn",
    )(u_bf, vt_bf, hft, cnt, e1, r2, e2, h1, gain)


def _tiles(B, S):
    ts_in = min(S, 512)
    ts_gdn = min(S, 256)
    tt = min(B * S, 512)
    eb = 2048
    return ts_in, ts_gdn, tt, eb


def kernel(x, norm_mix_gain, w_in, conv_w, a_log, dt_bias, gdn_out_gain, w_pool, b_pool, pool_scale,
           w_out, norm_ffn_gain, peer_w_query, peer_sub_keys_1, peer_sub_keys_2, peer_expert_u,
           peer_expert_v, norm_final_gain):
    B, S, D = x.shape
    depth = w_in.shape[0]
    pw = pool_scale.shape[1]
    gw = GDN_HEADS * GDN_HEAD_DIM
    H = GDN_HEADS
    ts_in, ts_gdn, tt, eb = _tiles(B, S)
    assert S % ts_in == 0 and S % ts_gdn == 0 and ts_gdn % CHUNK == 0 and (B * S) % tt == 0
    bf = jnp.bfloat16
    sel, selr = _gdn_constants()

    h = x
    for l in range(depth):
        wl = w_in[l]
        wp = wl[:, 0:pw].astype(bf)
        wqkv = wl[:, pw:pw + 3 * gw].astype(bf)
        wz = wl[:, pw + 3 * gw:pw + 4 * gw].astype(bf)
        wba = jnp.pad(wl[:, pw + 4 * gw:pw + 4 * gw + 2 * H], ((0, 0), (0, LANES - 2 * H))).astype(bf)
        alog_pad = jnp.pad(a_log[l], (H, LANES - 2 * H))[None, :]
        dtb_pad = jnp.pad(dt_bias[l], (H, LANES - 2 * H))[None, :]
        pool, q, k, v, gate, bg = _mix_in(
            h, norm_mix_gain[l][None, :], wp, wqkv, wz, wba, conv_w[l], alog_pad, dtb_pad,
            w_pool[l].astype(bf), b_pool[l].reshape(1, pw), pool_scale[l][None, :], ts=ts_in)
        h1 = _gdn_out(h, pool, q, k, v, gate, bg, jnp.tile(gdn_out_gain[l], H)[None, :],
                      w_out[l].astype(bf), sel, selr, ts=ts_gdn)

        h1f = h1.reshape(B * S, D)
        hft, cnt, e1, r2, e2 = _peer_prep(
            h1f, norm_ffn_gain[l][None, :], peer_w_query[l].T.astype(bf),
            peer_sub_keys_1[l], peer_sub_keys_2[l], tt=tt)
        assert depth == 1
        out = _peer_ffn(peer_expert_u[l].astype(bf), peer_expert_v[l].T.astype(bf), hft, cnt, e1, r2, e2,
                        h1f, norm_final_gain[None, :], tt=tt, eb=eb)
        h = out.reshape(B, S, D)
    return h
```

```python
import functools
import math

import jax
import jax.numpy as jnp
import numpy as np
from jax import lax
from jax.experimental import pallas as pl
from jax.experimental.pallas import tpu as pltpu

NORM_EPS = 1e-6
CHUNK = 64
POOL_WINDOWS = (2, 4, 8, 16)
CONV_WIDTH = 4
GDN_HEADS = 4
GDN_HEAD_DIM = 128
PEER_HEADS = 8
PEER_N_KEYS = 128
PEER_HALF = 128
PEER_TOPK = 16
PEER_SUB = 256

LANES = 128
POOL_CARRY = 16
CONV_CARRY = 8
VMEM_LIMIT = 56 * 1024 * 1024


def _dot(a, b):
    return jnp.dot(a, b, preferred_element_type=jnp.float32)


def _dot_nt(a, b):
    return lax.dot_general(a, b, (((1,), (1,)), ((), ())), preferred_element_type=jnp.float32)


def _dot_tn(a, b):
    return lax.dot_general(a, b, (((0,), (0,)), ((), ())), preferred_element_type=jnp.float32)


def _rms_rows(x, gain):
    return x * lax.rsqrt(jnp.mean(x * x, axis=-1, keepdims=True) + NORM_EPS) * gain


def _mix_in_kernel(x_ref, gain_ref, wp_ref, wqkv_ref, wz_ref, wba_ref, convw_ref, alog_ref, dtb_ref,
                   wpool_ref, bpool_ref, pscale_ref,
                   pool_ref, q_ref, k_ref, v_ref, gate_ref, bg_ref,
                   pbuf, cbuf, *, ts):
    s = pl.program_id(1)

    @pl.when(s == 0)
    def _():
        pbuf[0:POOL_CARRY, :] = jnp.zeros((POOL_CARRY, pbuf.shape[1]), jnp.float32)
        cbuf[0:CONV_CARRY, :] = jnp.zeros((CONV_CARRY, cbuf.shape[1]), jnp.float32)

    @pl.when(s > 0)
    def _():
        pbuf[0:POOL_CARRY, :] = pbuf[ts:ts + POOL_CARRY, :]
        cbuf[0:CONV_CARRY, :] = cbuf[ts:ts + CONV_CARRY, :]

    x = x_ref[0]
    hn = _rms_rows(x, gain_ref[...]).astype(jnp.bfloat16)

    p = _dot(hn, wp_ref[...])
    pbuf[POOL_CARRY:POOL_CARRY + ts, :] = p
    pos1 = (lax.broadcasted_iota(jnp.int32, (ts, LANES), 0) + s * ts + 1).astype(jnp.float32)
    for g, win in enumerate(POOL_WINDOWS):
        cols = slice(g * LANES, (g + 1) * LANES)
        ws = pbuf[POOL_CARRY:POOL_CARRY + ts, cols]
        for j in range(1, win):
            ws = ws + pbuf[POOL_CARRY - j:POOL_CARRY - j + ts, cols]
        d = ws / jnp.minimum(pos1, float(win)) - p[:, cols]
        y = _dot(d.astype(jnp.bfloat16), wpool_ref[g]) + bpool_ref[:, cols]
        pool_ref[0, :, cols] = y * pscale_ref[:, cols]

    cbuf[CONV_CARRY:CONV_CARRY + ts, :] = _dot(hn, wqkv_ref[...])
    acc = convw_ref[0:1, :] * cbuf[CONV_CARRY - 3:CONV_CARRY - 3 + ts, :]
    for j in range(1, CONV_WIDTH):
        acc = acc + convw_ref[j:j + 1, :] * cbuf[CONV_CARRY - 3 + j:CONV_CARRY - 3 + j + ts, :]
    qkv = acc * jax.nn.sigmoid(acc)
    width = GDN_HEADS * GDN_HEAD_DIM
    for h in range(GDN_HEADS):
        cq = slice(h * GDN_HEAD_DIM, (h + 1) * GDN_HEAD_DIM)
        ck = slice(width + h * GDN_HEAD_DIM, width + (h + 1) * GDN_HEAD_DIM)
        cv = slice(2 * width + h * GDN_HEAD_DIM, 2 * width + (h + 1) * GDN_HEAD_DIM)
        qh = qkv[:, cq]
        kh = qkv[:, ck]
        q_ref[0, :, cq] = qh * lax.rsqrt(jnp.sum(qh * qh, axis=-1, keepdims=True) + NORM_EPS) * (GDN_HEAD_DIM ** -0.5)
        k_ref[0, :, cq] = kh * lax.rsqrt(jnp.sum(kh * kh, axis=-1, keepdims=True) + NORM_EPS)
        v_ref[0, :, cq] = qkv[:, cv]

    z = _dot(hn, wz_ref[...])
    gate_ref[0] = z * jax.nn.sigmoid(z)

    ba = _dot(hn, wba_ref[...])
    lane = lax.broadcasted_iota(jnp.int32, ba.shape, 1)
    sp_in = ba + dtb_ref[...]
    softplus = jnp.maximum(sp_in, 0.0) + jnp.log1p(jnp.exp(-jnp.abs(sp_in)))
    gdec = -jnp.exp(alog_ref[...]) * softplus
    bg_ref[0] = jnp.where(lane < GDN_HEADS, jax.nn.sigmoid(ba),
                          jnp.where(lane < 2 * GDN_HEADS, gdec, 0.0))


def _mix_in(x, gain, wp, wqkv, wz, wba, convw, alog_pad, dtb_pad, wpool, bpool, pscale, *, ts):
    B, S, D = x.shape
    pw = wp.shape[1]
    gw = wz.shape[1]
    full = lambda shape: pl.BlockSpec(shape, lambda b, s: (0,) * len(shape))
    tile = lambda w: pl.BlockSpec((1, ts, w), lambda b, s: (b, s, 0))
    out_shapes = (
        jax.ShapeDtypeStruct((B, S, pw), jnp.float32),
        jax.ShapeDtypeStruct((B, S, gw), jnp.float32),
        jax.ShapeDtypeStruct((B, S, gw), jnp.float32),
        jax.ShapeDtypeStruct((B, S, gw), jnp.float32),
        jax.ShapeDtypeStruct((B, S, gw), jnp.float32),
        jax.ShapeDtypeStruct((B, S, LANES), jnp.float32),
    )
    return pl.pallas_call(
        functools.partial(_mix_in_kernel, ts=ts),
        grid=(B, S // ts),
        in_specs=[tile(D), full(gain.shape), full(wp.shape), full(wqkv.shape), full(wz.shape),
                  full(wba.shape), full(convw.shape), full(alog_pad.shape), full(dtb_pad.shape),
                  full(wpool.shape), full(bpool.shape), full(pscale.shape)],
        out_specs=(tile(pw), tile(gw), tile(gw), tile(gw), tile(gw), tile(LANES)),
        out_shape=out_shapes,
        scratch_shapes=[pltpu.VMEM((POOL_CARRY + ts, pw), jnp.float32),
                        pltpu.VMEM((CONV_CARRY + ts, wqkv.shape[1]), jnp.float32)],
        compiler_params=pltpu.CompilerParams(
            dimension_semantics=("arbitrary", "arbitrary"), vmem_limit_bytes=VMEM_LIMIT),
        name="mix_in",
    )(x, gain, wp, wqkv, wz, wba, convw, alog_pad, dtb_pad, wpool, bpool, pscale)


def _split3(a):
    p1 = a.astype(jnp.bfloat16)
    r1 = a - p1.astype(jnp.float32)
    p2 = r1.astype(jnp.bfloat16)
    p3 = (r1 - p2.astype(jnp.float32)).astype(jnp.bfloat16)
    return p1, p2, p3


def _gdn_out_kernel(x_ref, pool_ref, q_ref, k_ref, v_ref, gate_ref, bg_ref, ogain_ref, wout_ref,
                    sel_ref, selr_ref,
                    h_ref, state, o_buf, *, ts):
    s = pl.program_id(1)
    H, Dh, C = GDN_HEADS, GDN_HEAD_DIM, CHUNK
    R = H * C
    bf = jnp.bfloat16

    @pl.when(s == 0)
    def _():
        state[...] = jnp.zeros(state.shape, jnp.float32)

    ri = lax.broadcasted_iota(jnp.int32, (R, R), 0)
    ci = lax.broadcasted_iota(jnp.int32, (R, R), 1)
    same = (ri // C) == (ci // C)
    causal = same & (ri >= ci)
    strict = same & (ri > ci)
    eye = (ri == ci).astype(jnp.float32)

    bg = bg_ref[0]
    trow = lax.broadcasted_iota(jnp.int32, (ts, LANES), 0) % C
    tlane = lax.broadcasted_iota(jnp.int32, (ts, LANES), 1)
    cum = bg
    step = 1
    while step < C:
        cum = cum + jnp.where(trow >= step, pltpu.roll(cum, step, axis=0), 0.0)
        step *= 2
    tot = jnp.concatenate(
        [jnp.broadcast_to(cum[c * C + C - 1:c * C + C, :], (C, LANES)) for c in range(ts // C)], axis=0)
    feat = jnp.where(tlane < H, bg,
                     jnp.where(tlane < 2 * H, cum,
                               jnp.where(tlane < 3 * H, pltpu.roll(tot, H, axis=1), 0.0)))
    lane = lax.broadcasted_iota(jnp.int32, (C, LANES), 1)

    n_chunks = ts // C
    stacked = lambda ref, rows: jnp.concatenate([ref[0, rows, h * Dh:(h + 1) * Dh] for h in range(H)], axis=0)

    pre = []
    for c in range(n_chunks):
        rows = slice(c * C, (c + 1) * C)
        fc = feat[rows]
        fm = jnp.concatenate([jnp.where(lane % H == h, fc, 0.0) for h in range(H)], axis=0)
        fcat = jnp.concatenate(_split3(fm), axis=1)
        bc = _dot(fcat, sel_ref[...])
        gcum_cols = _dot_nt(selr_ref[...], fcat)
        pre.append(dict(rows=rows, bc=bc, gcum_cols=gcum_cols))
    for p in pre:
        bc = p["bc"]
        p["gcum"] = bc[:, 0:Dh]
        p["gtot"] = bc[:, R + Dh:R + 2 * Dh]
        diff = bc[:, 0:R] - p["gcum_cols"]
        p["dec"] = jnp.where(causal, jnp.exp(jnp.where(causal, diff, 0.0)), 0.0)
        beta_b = bc[:, R:R + Dh]
        ks = stacked(k_ref, p["rows"])
        p["ks"] = ks
        p["ks_bf"] = ks.astype(bf)
        p["kb"] = ks * beta_b
        p["vb"] = stacked(v_ref, p["rows"]) * beta_b
        kk = _dot_nt(p["kb"].astype(bf), p["ks_bf"])
        m = jnp.where(strict, -(kk * p["dec"]), 0.0)
        p["tinv"] = eye + m
        p["pw"] = m
    for _ in range(int(math.log2(C)) - 1):
        for p in pre:
            pw_bf = p["pw"].astype(bf)
            p["pw"] = _dot(pw_bf, pw_bf)
        for p in pre:
            p["tinv"] = p["tinv"] + _dot(p["pw"].astype(bf), p["tinv"].astype(bf))
    for p in pre:
        egc = jnp.exp(p["gcum"])
        sol = _dot(p["tinv"].astype(bf), jnp.concatenate([p["vb"], p["kb"] * egc], axis=1).astype(bf))
        qs = stacked(q_ref, p["rows"])
        p["u"] = sol[:, 0:Dh]
        p["w"] = sol[:, Dh:2 * Dh].astype(bf)
        p["qk"] = (_dot_nt(qs.astype(bf), p["ks_bf"]) * p["dec"]).astype(bf)
        p["qd"] = (qs * egc).astype(bf)
        p["kd"] = (p["ks"] * jnp.exp(p["gtot"] - p["gcum"])).astype(bf)
        p["cd"] = jnp.exp(p["gtot"])

    st = [state[h] for h in range(H)]
    for p in pre:
        st_bf = [t.astype(bf) for t in st]
        u, w, qd, kd, cd = p["u"], p["w"], p["qd"], p["kd"], p["cd"]
        vnew = jnp.concatenate(
            [u[h * C:(h + 1) * C] - _dot(w[h * C:(h + 1) * C], st_bf[h]) for h in range(H)], axis=0)
        vnew_bf = vnew.astype(bf)
        o_intra = _dot(p["qk"], vnew_bf)
        for h in range(H):
            hr = slice(h * C, (h + 1) * C)
            o_buf[p["rows"], h * Dh:(h + 1) * Dh] = _dot(qd[hr], st_bf[h]) + o_intra[hr]
            st[h] = st[h] * cd[h * C:h * C + 1, :] + _dot_tn(kd[hr], vnew_bf[hr])
    for h in range(H):
        state[h] = st[h]

    o = o_buf[...]
    gdn = jnp.concatenate(
        [o[:, h * Dh:(h + 1) * Dh]
         * lax.rsqrt(jnp.mean(o[:, h * Dh:(h + 1) * Dh] * o[:, h * Dh:(h + 1) * Dh], axis=-1, keepdims=True) + NORM_EPS)
         for h in range(H)], axis=1)
    gdn = gdn * ogain_ref[...] * gate_ref[0]
    mixed = jnp.concatenate([pool_ref[0], gdn], axis=1).astype(jnp.bfloat16)
    h_ref[0] = x_ref[0] + _dot(mixed, wout_ref[...])


def _gdn_out(x, pool, q, k, v, gate, bg, ogain, wout, sel, selr, *, ts):
    B, S, D = x.shape
    gw = q.shape[2]
    full = lambda shape: pl.BlockSpec(shape, lambda b, s: (0,) * len(shape))
    tile = lambda w: pl.BlockSpec((1, ts, w), lambda b, s: (b, s, 0))
    return pl.pallas_call(
        functools.partial(_gdn_out_kernel, ts=ts),
        grid=(B, S // ts),
        in_specs=[tile(D), tile(pool.shape[2]), tile(gw), tile(gw), tile(gw), tile(gw), tile(LANES),
                  full(ogain.shape), full(wout.shape), full(sel.shape), full(selr.shape)],
        out_specs=tile(D),
        out_shape=jax.ShapeDtypeStruct((B, S, D), jnp.float32),
        scratch_shapes=[pltpu.VMEM((GDN_HEADS, GDN_HEAD_DIM, GDN_HEAD_DIM), jnp.float32),
                        pltpu.VMEM((ts, gw), jnp.float32)],
        compiler_params=pltpu.CompilerParams(
            dimension_semantics=("arbitrary", "arbitrary"), vmem_limit_bytes=VMEM_LIMIT),
        name="gdn_out",
    )(x, pool, q, k, v, gate, bg, ogain, wout, sel, selr)


def _gdn_constants():
    H, C, Dh = GDN_HEADS, CHUNK, GDN_HEAD_DIM
    R = H * C
    sel = np.zeros((3 * LANES, R + 2 * Dh), np.float32)
    selr = np.zeros((R, 3 * LANES), np.float32)
    for piece in range(3):
        base = piece * LANES
        sel[base + H:base + 2 * H, 0:R] = 1.0
        sel[base:base + H, R:R + Dh] = 1.0
        sel[base + 2 * H:base + 3 * H, R + Dh:R + 2 * Dh] = 1.0
        selr[:, base + H:base + 2 * H] = 1.0
    return jnp.asarray(sel, jnp.bfloat16), jnp.asarray(selr, jnp.bfloat16)


def _top_values(x, n, n_ranked=0):
    tops = []
    rank = jnp.full(x.shape, float(n_ranked), jnp.float32)
    for i in range(n):
        m = jnp.max(x, axis=0, keepdims=True)
        tops.append(m)
        last = i + 1 == n
        if not last or i < n_ranked:
            hit = x == m
            if i < n_ranked:
                rank = jnp.where(hit, float(i), rank)
            if not last:
                x = jnp.where(hit, -jnp.inf, x)
    return tops, rank


def _peer_prep_kernel(h_ref, gain_ref, wqt_ref, k1_ref, k2_ref,
                      hft_ref, cnt_ref, e1_ref, r2_ref, e2_ref):
    NH, K = PEER_HEADS, PEER_TOPK
    hf = _rms_rows(h_ref[...], gain_ref[...])
    hft = hf.T.astype(jnp.bfloat16)
    hft_ref[...] = hft
    qt = _dot(wqt_ref[...], hft)

    s1, s2, a_top, b_top = [], [], [], []
    for h in range(NH):
        base = h * 2 * PEER_HALF
        s1h = _dot(k1_ref[h], qt[base:base + PEER_HALF])
        s2h = _dot(k2_ref[h], qt[base + PEER_HALF:base + 2 * PEER_HALF])
        s1.append(s1h)
        s2.append(s2h)
        a_top.append(_top_values(s1h, K)[0])
        tops, rank = _top_values(s2h, K, K)
        b_top.append(tops)
        r2_ref[h] = rank.astype(jnp.bfloat16)

    A = [jnp.concatenate([a_top[h][i] for h in range(NH)], axis=0) for i in range(K)]
    Bv = [jnp.concatenate([b_top[h][j] for h in range(NH)], axis=0) for j in range(K)]
    cells = [(i, j) for i in range(K) for j in range(K) if (i + 1) * (j + 1) <= K]
    cand = {(i, j): A[i] + Bv[j] for (i, j) in cells}
    work = list(cand.values())
    for r in range(K):
        m = work[0]
        for cnd in work[1:]:
            m = jnp.maximum(m, cnd)
        if r + 1 < K:
            work = [jnp.where(cnd == m, -jnp.inf, cnd) for cnd in work]
    tau = m
    top = A[0] + Bv[0]
    z = jnp.zeros_like(tau)
    counts = [jnp.zeros_like(tau) for _ in range(K)]
    for (i, j), cnd in cand.items():
        hit = cnd >= tau
        z = z + jnp.where(hit, jnp.exp(cnd - top), 0.0)
        counts[i] = counts[i] + jnp.where(hit, 1.0, 0.0)
    zinv = 0.5 / z

    for h in range(NH):
        cnt = jnp.zeros_like(s1[h])
        for i in range(K):
            cnt = jnp.where(s1[h] == a_top[h][i], counts[i][h:h + 1, :], cnt)
        cnt_ref[h] = cnt
        e1_ref[h] = jnp.exp(s1[h] - A[0][h:h + 1, :]) * zinv[h:h + 1, :]
        e2_ref[h] =jnp.exp(s2[h] - Bv[0][h:h + 1, :]).astype(jnp.bfloat16)


def _peer_prep(h1, gain, wqt, k1, k2, *, tt):
    N, D = h1.shape
    NH, NK = k1.shape[0], k1.shape[1]
    full = lambda shape: pl.BlockSpec(shape, lambda i: (0,) * len(shape))
    head_out = pl.BlockSpec((NH, NK, tt), lambda i: (0, 0, i))
    row_shape = jax.ShapeDtypeStruct((NH, NK, N), jnp.float32)
    half_shape = jax.ShapeDtypeStruct((NH, NK, N), jnp.bfloat16)
    return pl.pallas_call(
        _peer_prep_kernel,
        grid=(N // tt,),
        in_specs=[pl.BlockSpec((tt, D), lambda i: (i, 0)), full(gain.shape), full(wqt.shape),
                  full(k1.shape), full(k2.shape)],
        out_specs=(pl.BlockSpec((D, tt), lambda i: (0, i)), head_out, head_out, head_out, head_out),
        out_shape=(jax.ShapeDtypeStruct((D, N), jnp.bfloat16), row_shape, row_shape, half_shape, half_shape),
        compiler_params=pltpu.CompilerParams(
            dimension_semantics=("arbitrary",), vmem_limit_bytes=VMEM_LIMIT),
        name="peer_prep",
    )(h1, gain, wqt, k1, k2)


def _peer_ffn_kernel(u_ref, vt_ref, hft_ref, cnt_ref, e1_ref, r2_ref, e2_ref, h_ref, gain_ref,
                     out_ref, acc, *, eb):
    j = pl.program_id(1)
    NH, NK = PEER_HEADS, PEER_N_KEYS
    tt = hft_ref.shape[1]
    bf = jnp.bfloat16

    @pl.when(j == 0)
    def _():
        acc[...] = jnp.zeros(acc.shape, jnp.float32)

    def row_bcast(ref, h, a):
        return jnp.broadcast_to(ref[h, a:a + 1, :], (NK, tt)).astype(bf)

    def activations(sb):
        return _dot(u_ref[sb * PEER_SUB:(sb + 1) * PEER_SUB, :], hft_ref[...])

    def gates(sb):
        blocks = []
        for a in range(sb * (PEER_SUB // NK), (sb + 1) * (PEER_SUB // NK)):
            g = None
            for h in range(NH):
                hit = r2_ref[h] < row_bcast(cnt_ref, h, a)
                term = jnp.where(hit, e2_ref[h] * row_bcast(e1_ref, h, a), jnp.zeros((), bf))
                g = term if g is None else g + term
            blocks.append(g)
        return jnp.concatenate(blocks, axis=0)

    n_sub = eb // PEER_SUB
    act_cur, g_cur = activations(0), gates(0)
    for sb in range(n_sub):
        r0 = sb * PEER_SUB
        if sb + 1 < n_sub:
            act_next = activations(sb + 1)
        gelu2 = act_cur * (1.0 + lax.erf(act_cur * np.float32(math.sqrt(0.5))))
        acc[...] += _dot(vt_ref[:, r0:r0 + PEER_SUB], g_cur * gelu2.astype(bf))
        if sb + 1 < n_sub:
            act_cur, g_cur = act_next, gates(sb + 1)

    @pl.when(j == pl.num_programs(1) - 1)
    def _():
        h2 = h_ref[...] + acc[...].T
        out_ref[...] = _rms_rows(h2, gain_ref[...])


def _peer_ffn(u_bf, vt_bf, hft, cnt, e1, r2, e2, h1, gain, *, tt, eb):
    N, D = h1.shape
    E = u_bf.shape[0]
    NH, NK = PEER_HEADS, PEER_N_KEYS
    ab = eb // NK
    return pl.pallas_call(
        functools.partial(_peer_ffn_kernel, eb=eb),
        grid=(N // tt, E // eb),
        in_specs=[pl.BlockSpec((eb, D), lambda i, j: (j, 0)),
                  pl.BlockSpec((D, eb), lambda i, j: (0, j)),
                  pl.BlockSpec((D, tt), lambda i, j: (0, i)),
                  pl.BlockSpec((NH, ab, tt), lambda i, j: (0, j, i)),
                  pl.BlockSpec((NH, ab, tt), lambda i, j: (0, j, i)),
                  pl.BlockSpec((NH, NK, tt), lambda i, j: (0, 0, i)),
                  pl.BlockSpec((NH, NK, tt), lambda i, j: (0, 0, i)),
                  pl.BlockSpec((tt, D), lambda i, j: (i, 0)),
                  pl.BlockSpec(gain.shape, lambda i, j: (0, 0))],
        out_specs=pl.BlockSpec((tt, D), lambda i, j: (i, 0)),
        out_shape=jax.ShapeDtypeStruct((N, D), jnp.float32),
        scratch_shapes=[pltpu.VMEM((D, tt), jnp.float32)],
        compiler_params=pltpu.CompilerParams(
            dimension_semantics=("arbitrary", "arbitrary"), vmem_limit_bytes=VMEM_LIMIT),
        name="peer_ffn",
    )(u_bf, vt_bf, hft, cnt, e1, r2, e2, h1, gain)


def _tiles(B, S):
    ts_in = min(S, 512)
    ts_gdn = min(S, 512)
    tt = min(B * S, 512)
    eb = 2048
    return ts_in, ts_gdn, tt, eb


def kernel(x, norm_mix_gain, w_in, conv_w, a_log, dt_bias, gdn_out_gain, w_pool, b_pool, pool_scale,
           w_out, norm_ffn_gain, peer_w_query, peer_sub_keys_1, peer_sub_keys_2, peer_expert_u,
           peer_expert_v, norm_final_gain):
    B, S, D = x.shape
    depth = w_in.shape[0]
    pw = pool_scale.shape[1]
    gw = GDN_HEADS * GDN_HEAD_DIM
    H = GDN_HEADS
    ts_in, ts_gdn, tt, eb = _tiles(B, S)
    assert S % ts_in == 0 and S % ts_gdn == 0 and ts_gdn % CHUNK == 0 and (B * S) % tt == 0
    bf = jnp.bfloat16
    sel, selr = _gdn_constants()

    h = x
    for l in range(depth):
        wl = w_in[l]
        wp = wl[:, 0:pw].astype(bf)
        wqkv = wl[:, pw:pw + 3 * gw].astype(bf)
        wz = wl[:, pw + 3 * gw:pw + 4 * gw].astype(bf)
        wba = jnp.pad(wl[:, pw + 4 * gw:pw + 4 * gw + 2 * H], ((0, 0), (0, LANES - 2 * H))).astype(bf)
        alog_pad = jnp.pad(a_log[l], (H, LANES - 2 * H))[None, :]
        dtb_pad = jnp.pad(dt_bias[l], (H, LANES - 2 * H))[None, :]
        pool, q, k, v, gate, bg = _mix_in(
            h, norm_mix_gain[l][None, :], wp, wqkv, wz, wba, conv_w[l], alog_pad, dtb_pad,
            w_pool[l].astype(bf), b_pool[l].reshape(1, pw), pool_scale[l][None, :], ts=ts_in)
        h1 = _gdn_out(h, pool, q, k, v, gate, bg, jnp.tile(gdn_out_gain[l], H)[None, :],
                      w_out[l].astype(bf), sel, selr, ts=ts_gdn)

        h1f = h1.reshape(B * S, D)
        hft, cnt, e1, r2, e2 = _peer_prep(
            h1f, norm_ffn_gain[l][None, :], peer_w_query[l].T.astype(bf),
            peer_sub_keys_1[l], peer_sub_keys_2[l], tt=tt)
        assert depth == 1
        out = _peer_ffn(peer_expert_u[l].astype(bf), peer_expert_v[l].T.astype(bf), hft, cnt, e1, r2, e2,
                        h1f, norm_final_gain[None, :], tt=tt, eb=eb)
        h = out.reshape(B, S, D)
    return h
```

```python
import functools
import math

import jax
import jax.numpy as jnp
import numpy as np
from jax import lax
from jax.experimental import pallas as pl
from jax.experimental.pallas import tpu as pltpu

NORM_EPS = 1e-6
CHUNK = 64
POOL_WINDOWS = (2, 4, 8, 16)
CONV_WIDTH = 4
GDN_HEADS = 4
GDN_HEAD_DIM = 128
PEER_HEADS = 8
PEER_N_KEYS = 128
PEER_HALF = 128
PEER_TOPK = 16
PEER_SUB = 256

LANES = 128
SUBLANES = 8
POOL_CARRY = 16
CONV_CARRY = 8
VMEM_LIMIT = 56 * 1024 * 1024


def _dot(a, b):
    return jnp.dot(a, b, preferred_element_type=jnp.float32)


def _dot_nt(a, b):
    return lax.dot_general(a, b, (((1,), (1,)), ((), ())), preferred_element_type=jnp.float32)


def _dot_tn(a, b):
    return lax.dot_general(a, b, (((0,), (0,)), ((), ())), preferred_element_type=jnp.float32)


def _rms_rows(x, gain):
    return x * lax.rsqrt(jnp.mean(x * x, axis=-1, keepdims=True) + NORM_EPS) * gain


def _mix_in_kernel(x_ref, gain_ref, wp_ref, wqkv_ref, wz_ref, wba_ref, convw_ref, alog_ref, dtb_ref,
                   wpool_ref, bpool_ref, pscale_ref,
                   pool_ref, q_ref, k_ref, v_ref, gate_ref, bg_ref,
                   pbuf, cbuf, *, ts):
    s = pl.program_id(1)

    @pl.when(s == 0)
    def _():
        pbuf[0:POOL_CARRY, :] = jnp.zeros((POOL_CARRY, pbuf.shape[1]), jnp.float32)
        cbuf[0:CONV_CARRY, :] = jnp.zeros((CONV_CARRY, cbuf.shape[1]), jnp.float32)

    @pl.when(s > 0)
    def _():
        pbuf[0:POOL_CARRY, :] = pbuf[ts:ts + POOL_CARRY, :]
        cbuf[0:CONV_CARRY, :] = cbuf[ts:ts + CONV_CARRY, :]

    x = x_ref[0]
    hn = _rms_rows(x, gain_ref[...]).astype(jnp.bfloat16)

    p = _dot(hn, wp_ref[...])
    pbuf[POOL_CARRY:POOL_CARRY + ts, :] = p
    pos1 = (lax.broadcasted_iota(jnp.int32, (ts, LANES), 0) + s * ts + 1).astype(jnp.float32)
    for g, win in enumerate(POOL_WINDOWS):
        cols = slice(g * LANES, (g + 1) * LANES)
        ws = pbuf[POOL_CARRY:POOL_CARRY + ts, cols]
        for j in range(1, win):
            ws = ws + pbuf[POOL_CARRY - j:POOL_CARRY - j + ts, cols]
        d = ws / jnp.minimum(pos1, float(win)) - p[:, cols]
        y = _dot(d.astype(jnp.bfloat16), wpool_ref[g]) + bpool_ref[:, cols]
        pool_ref[0, :, cols] = y * pscale_ref[:, cols]

    cbuf[CONV_CARRY:CONV_CARRY + ts, :] = _dot(hn, wqkv_ref[...])
    acc = convw_ref[0:1, :] * cbuf[CONV_CARRY - 3:CONV_CARRY - 3 + ts, :]
    for j in range(1, CONV_WIDTH):
        acc = acc + convw_ref[j:j + 1, :] * cbuf[CONV_CARRY - 3 + j:CONV_CARRY - 3 + j + ts, :]
    qkv = acc * jax.nn.sigmoid(acc)
    width = GDN_HEADS * GDN_HEAD_DIM
    for h in range(GDN_HEADS):
        cq = slice(h * GDN_HEAD_DIM, (h + 1) * GDN_HEAD_DIM)
        ck = slice(width + h * GDN_HEAD_DIM, width + (h + 1) * GDN_HEAD_DIM)
        cv = slice(2 * width + h * GDN_HEAD_DIM, 2 * width + (h + 1) * GDN_HEAD_DIM)
        qh = qkv[:, cq]
        kh = qkv[:, ck]
        q_ref[0, :, cq] = qh * lax.rsqrt(jnp.sum(qh * qh, axis=-1, keepdims=True) + NORM_EPS) * (GDN_HEAD_DIM ** -0.5)
        k_ref[0, :, cq] = kh * lax.rsqrt(jnp.sum(kh * kh, axis=-1, keepdims=True) + NORM_EPS)
        v_ref[0, :, cq] = qkv[:, cv]

    z = _dot(hn, wz_ref[...])
    gate_ref[0] = z * jax.nn.sigmoid(z)

    ba = _dot(hn, wba_ref[...])
    lane = lax.broadcasted_iota(jnp.int32, ba.shape, 1)
    sp_in = ba + dtb_ref[...]
    softplus = jnp.maximum(sp_in, 0.0) + jnp.log1p(jnp.exp(-jnp.abs(sp_in)))
    gdec = -jnp.exp(alog_ref[...]) * softplus
    bg_ref[0] = jnp.where(lane < GDN_HEADS, jax.nn.sigmoid(ba),
                          jnp.where(lane < 2 * GDN_HEADS, gdec, 0.0))


def _mix_in(x, gain, wp, wqkv, wz, wba, convw, alog_pad, dtb_pad, wpool, bpool, pscale, *, ts):
    B, S, D = x.shape
    pw = wp.shape[1]
    gw = wz.shape[1]
    full = lambda shape: pl.BlockSpec(shape, lambda b, s: (0,) * len(shape))
    tile = lambda w: pl.BlockSpec((1, ts, w), lambda b, s: (b, s, 0))
    out_shapes = (
        jax.ShapeDtypeStruct((B, S, pw), jnp.float32),
        jax.ShapeDtypeStruct((B, S, gw), jnp.float32),
        jax.ShapeDtypeStruct((B, S, gw), jnp.float32),
        jax.ShapeDtypeStruct((B, S, gw), jnp.float32),
        jax.ShapeDtypeStruct((B, S, gw), jnp.float32),
        jax.ShapeDtypeStruct((B, S, LANES), jnp.float32),
    )
    return pl.pallas_call(
        functools.partial(_mix_in_kernel, ts=ts),
        grid=(B, S // ts),
        in_specs=[tile(D), full(gain.shape), full(wp.shape), full(wqkv.shape), full(wz.shape),
                  full(wba.shape), full(convw.shape), full(alog_pad.shape), full(dtb_pad.shape),
                  full(wpool.shape), full(bpool.shape), full(pscale.shape)],
        out_specs=(tile(pw), tile(gw), tile(gw), tile(gw), tile(gw), tile(LANES)),
        out_shape=out_shapes,
        scratch_shapes=[pltpu.VMEM((POOL_CARRY + ts, pw), jnp.float32),
                        pltpu.VMEM((CONV_CARRY + ts, wqkv.shape[1]), jnp.float32)],
        compiler_params=pltpu.CompilerParams(
            dimension_semantics=("arbitrary", "arbitrary"), vmem_limit_bytes=VMEM_LIMIT),
        name="mix_in",
    )(x, gain, wp, wqkv, wz, wba, convw, alog_pad, dtb_pad, wpool, bpool, pscale)


def _split3(a):
    p1 = a.astype(jnp.bfloat16)
    r1 = a - p1.astype(jnp.float32)
    p2 = r1.astype(jnp.bfloat16)
    p3 = (r1 - p2.astype(jnp.float32)).astype(jnp.bfloat16)
    return p1, p2, p3


def _gdn_out_kernel(x_ref, pool_ref, q_ref, k_ref, v_ref, gate_ref, bg_ref, ogain_ref, wout_ref,
                    sel_ref, selr_ref,
                    h_ref, state, o_buf, *, ts):
    s = pl.program_id(1)
    H, Dh, C = GDN_HEADS, GDN_HEAD_DIM, CHUNK
    R = H * C
    bf = jnp.bfloat16

    @pl.when(s == 0)
    def _():
        state[...] = jnp.zeros(state.shape, jnp.float32)

    ri = lax.broadcasted_iota(jnp.int32, (R, R), 0)
    ci = lax.broadcasted_iota(jnp.int32, (R, R), 1)
    same = (ri // C) == (ci // C)
    causal = same & (ri >= ci)
    strict = same & (ri > ci)
    eye = (ri == ci).astype(jnp.float32)

    bg = bg_ref[0]
    trow = lax.broadcasted_iota(jnp.int32, (ts, LANES), 0) % C
    tlane = lax.broadcasted_iota(jnp.int32, (ts, LANES), 1)
    cum = bg
    step = 1
    while step < C:
        cum = cum + jnp.where(trow >= step, pltpu.roll(cum, step, axis=0), 0.0)
        step *= 2
    tot = jnp.concatenate(
        [jnp.broadcast_to(cum[c * C + C - 1:c * C + C, :], (C, LANES)) for c in range(ts // C)], axis=0)
    feat = jnp.where(tlane < H, bg,
                     jnp.where(tlane < 2 * H, cum,
                               jnp.where(tlane < 3 * H, pltpu.roll(tot, H, axis=1), 0.0)))
    lane = lax.broadcasted_iota(jnp.int32, (C, LANES), 1)

    n_chunks = ts // C
    stacked = lambda ref, rows: jnp.concatenate([ref[0, rows, h * Dh:(h + 1) * Dh] for h in range(H)], axis=0)

    pre = []
    for c in range(n_chunks):
        rows = slice(c * C, (c + 1) * C)
        fc = feat[rows]
        fm = jnp.concatenate([jnp.where(lane % H == h, fc, 0.0) for h in range(H)], axis=0)
        fcat = jnp.concatenate(_split3(fm), axis=1)
        bc = _dot(fcat, sel_ref[...])
        gcum_cols = _dot_nt(selr_ref[...], fcat)
        pre.append(dict(rows=rows, bc=bc, gcum_cols=gcum_cols))
    for p in pre:
        bc = p["bc"]
        p["gcum"] = bc[:, 0:Dh]
        p["gtot"] = bc[:, R + Dh:R + 2 * Dh]
        diff = bc[:, 0:R] - p["gcum_cols"]
        p["dec"] = jnp.where(causal, jnp.exp(jnp.where(causal, diff, 0.0)), 0.0)
        beta_b = bc[:, R:R + Dh]
        ks = stacked(k_ref, p["rows"])
        p["ks"] = ks
        p["ks_bf"] = ks.astype(bf)
        p["kb"] = ks * beta_b
        p["vb"] = stacked(v_ref, p["rows"]) * beta_b
        kk = _dot_nt(p["kb"].astype(bf), p["ks_bf"])
        m = jnp.where(strict, -(kk * p["dec"]), 0.0)
        p["tinv"] = eye + m
        p["pw"] = m
    for _ in range(int(math.log2(C)) - 1):
        for p in pre:
            pw_bf = p["pw"].astype(bf)
            p["pw"] = _dot(pw_bf, pw_bf)
        for p in pre:
            p["tinv"] = p["tinv"] + _dot(p["pw"].astype(bf), p["tinv"].astype(bf))
    for p in pre:
        egc = jnp.exp(p["gcum"])
        sol = _dot(p["tinv"].astype(bf), jnp.concatenate([p["vb"], p["kb"] * egc], axis=1).astype(bf))
        qs = stacked(q_ref, p["rows"])
        p["u"] = sol[:, 0:Dh]
        p["w"] = sol[:, Dh:2 * Dh].astype(bf)
        p["qk"] = (_dot_nt(qs.astype(bf), p["ks_bf"]) * p["dec"]).astype(bf)
        p["qd"] = (qs * egc).astype(bf)
        p["kd"] = (p["ks"] * jnp.exp(p["gtot"] - p["gcum"])).astype(bf)
        p["cd"] = jnp.exp(p["gtot"])

    st = [state[h] for h in range(H)]
    for p in pre:
        st_bf = [t.astype(bf) for t in st]
        u, w, qd, kd, cd = p["u"], p["w"], p["qd"], p["kd"], p["cd"]
        vnew = jnp.concatenate(
            [u[h * C:(h + 1) * C] - _dot(w[h * C:(h + 1) * C], st_bf[h]) for h in range(H)], axis=0)
        vnew_bf = vnew.astype(bf)
        o_intra = _dot(p["qk"], vnew_bf)
        for h in range(H):
            hr = slice(h * C, (h + 1) * C)
            o_buf[p["rows"], h * Dh:(h + 1) * Dh] = _dot(qd[hr], st_bf[h]) + o_intra[hr]
            st[h] = st[h] * cd[h * C:h * C + 1, :] + _dot_tn(kd[hr], vnew_bf[hr])
    for h in range(H):
        state[h] = st[h]

    o = o_buf[...]
    gdn = jnp.concatenate(
        [o[:, h * Dh:(h + 1) * Dh]
         * lax.rsqrt(jnp.mean(o[:, h * Dh:(h + 1) * Dh] * o[:, h * Dh:(h + 1) * Dh], axis=-1, keepdims=True) + NORM_EPS)
         for h in range(H)], axis=1)
    gdn = gdn * ogain_ref[...] * gate_ref[0]
    mixed = jnp.concatenate([pool_ref[0], gdn], axis=1).astype(jnp.bfloat16)
    h_ref[0] = x_ref[0] + _dot(mixed, wout_ref[...])


def _gdn_out(x, pool, q, k, v, gate, bg, ogain, wout, sel, selr, *, ts):
    B, S, D = x.shape
    gw = q.shape[2]
    full = lambda shape: pl.BlockSpec(shape, lambda b, s: (0,) * len(shape))
    tile = lambda w: pl.BlockSpec((1, ts, w), lambda b, s: (b, s, 0))
    return pl.pallas_call(
        functools.partial(_gdn_out_kernel, ts=ts),
        grid=(B, S // ts),
        in_specs=[tile(D), tile(pool.shape[2]), tile(gw), tile(gw), tile(gw), tile(gw), tile(LANES),
                  full(ogain.shape), full(wout.shape), full(sel.shape), full(selr.shape)],
        out_specs=tile(D),
        out_shape=jax.ShapeDtypeStruct((B, S, D), jnp.float32),
        scratch_shapes=[pltpu.VMEM((GDN_HEADS, GDN_HEAD_DIM, GDN_HEAD_DIM), jnp.float32),
                        pltpu.VMEM((ts, gw), jnp.float32)],
        compiler_params=pltpu.CompilerParams(
            dimension_semantics=("arbitrary", "arbitrary"), vmem_limit_bytes=VMEM_LIMIT),
        name="gdn_out",
    )(x, pool, q, k, v, gate, bg, ogain, wout, sel, selr)


def _gdn_constants():
    H, C, Dh = GDN_HEADS, CHUNK, GDN_HEAD_DIM
    R = H * C
    sel = np.zeros((3 * LANES, R + 2 * Dh), np.float32)
    selr = np.zeros((R, 3 * LANES), np.float32)
    for piece in range(3):
        base = piece * LANES
        sel[base + H:base + 2 * H, 0:R] = 1.0
        sel[base:base + H, R:R + Dh] = 1.0
        sel[base + 2 * H:base + 3 * H, R + Dh:R + 2 * Dh] = 1.0
        selr[:, base + H:base + 2 * H] = 1.0
    return jnp.asarray(sel, jnp.bfloat16), jnp.asarray(selr, jnp.bfloat16)


def _sorting_network(n):
    def merge(lo, hi, r):
        step = r * 2
        if step < hi - lo:
            yield from merge(lo, hi, step)
            yield from merge(lo + r, hi, step)
            yield from ((i, i + r) for i in range(lo + r, hi - r, step))
        else:
            yield (lo, lo + r)

    def sort(lo, hi):
        if hi - lo >= 1:
            mid = lo + (hi - lo) // 2
            yield from sort(lo, mid)
            yield from sort(mid + 1, hi)
            yield from merge(lo, hi, 1)

    return tuple(sort(0, n - 1))


def _top_values(x, n):
    lists = [x[i * SUBLANES:(i + 1) * SUBLANES] for i in range(x.shape[0] // SUBLANES)]
    for i, j in _sorting_network(len(lists)):
        lists[i], lists[j] = jnp.maximum(lists[i], lists[j]), jnp.minimum(lists[i], lists[j])
    tops = []
    for r in range(n):
        m = jnp.max(lists[0], axis=0, keepdims=True)
        tops.append(m)
        depth = min(n - 1 - r, len(lists))
        if depth:
            hit = lists[0] == m
            for i in range(depth):
                nxt = lists[i + 1] if i + 1 < len(lists) else -jnp.inf
                lists[i] = jnp.where(hit, nxt, lists[i])
    return tops


def _ranks(x, tops):
    rank = jnp.zeros(x.shape, jnp.float32)
    for r, m in enumerate(tops):
        rank = jnp.where(x < m, float(r + 1), rank)
    return rank


def _peer_prep_kernel(h_ref, gain_ref, wqt_ref, k1_ref, k2_ref,
                      hft_ref, cnt_ref, e1_ref, r2_ref, e2_ref):
    NH, K = PEER_HEADS, PEER_TOPK
    hf = _rms_rows(h_ref[...], gain_ref[...])
    hft = hf.T.astype(jnp.bfloat16)
    hft_ref[...] = hft
    qt = _dot(wqt_ref[...], hft)

    s1, s2, a_top, b_top = [], [], [], []
    for h in range(NH):
        base = h * 2 * PEER_HALF
        s1h = _dot(k1_ref[h], qt[base:base + PEER_HALF])
        s2h = _dot(k2_ref[h], qt[base + PEER_HALF:base + 2 * PEER_HALF])
        s1.append(s1h)
        s2.append(s2h)
        a_top.append(_top_values(s1h, K))
        b_top.append(_top_values(s2h, K))
        r2_ref[h] = _ranks(s2h, b_top[h]).astype(jnp.bfloat16)

    A = [jnp.concatenate([a_top[h][i] for h in range(NH)], axis=0) for i in range(K)]
    Bv = [jnp.concatenate([b_top[h][j] for h in range(NH)], axis=0) for j in range(K)]
    cells = [(i, j) for i in range(K) for j in range(K) if (i + 1) * (j + 1) <= K]
    cand = {(i, j): A[i] + Bv[j] for (i, j) in cells}
    work = list(cand.values())
    for r in range(K):
        m = work[0]
        for cnd in work[1:]:
            m = jnp.maximum(m, cnd)
        if r + 1 < K:
            work = [jnp.where(cnd == m, -jnp.inf, cnd) for cnd in work]
    tau = m
    top = A[0] + Bv[0]
    z = jnp.zeros_like(tau)
    counts = [jnp.zeros_like(tau) for _ in range(K)]
    for (i, j), cnd in cand.items():
        hit = cnd >= tau
        z = z + jnp.where(hit, jnp.exp(cnd - top), 0.0)
        counts[i] = counts[i] + jnp.where(hit, 1.0, 0.0)
    zinv = 0.5 / z

    floors = []
    for c in range(1, K // 2 + 1):
        f = jnp.full_like(tau, jnp.inf)
        for i in range(1, K):
            f = jnp.minimum(f, jnp.where(counts[i] >= float(c), A[i], jnp.inf))
        floors.append(f)

    for h in range(NH):
        cnt = jnp.zeros_like(s1[h])
        for c, f in enumerate(floors):
            cnt = jnp.where(s1[h] >= f[h:h + 1, :], float(c + 1), cnt)
        cnt = jnp.where(s1[h] >= A[0][h:h + 1, :], counts[0][h:h + 1, :], cnt)
        cnt_ref[h] = cnt
        e1_ref[h] = jnp.exp(s1[h] - A[0][h:h + 1, :]) * zinv[h:h + 1, :]
        e2_ref[h] = jnp.exp(s2[h] - Bv[0][h:h + 1, :]).astype(jnp.bfloat16)


def _peer_prep(h1, gain, wqt, k1, k2, *, tt):
    N, D = h1.shape
    NH, NK = k1.shape[0], k1.shape[1]
    full = lambda shape: pl.BlockSpec(shape, lambda i: (0,) * len(shape))
    head_out = pl.BlockSpec((NH, NK, tt), lambda i: (0, 0, i))
    row_shape = jax.ShapeDtypeStruct((NH, NK, N), jnp.float32)
    half_shape = jax.ShapeDtypeStruct((NH, NK, N), jnp.bfloat16)
    return pl.pallas_call(
        _peer_prep_kernel,
        grid=(N // tt,),
        in_specs=[pl.BlockSpec((tt, D), lambda i: (i, 0)), full(gain.shape), full(wqt.shape),
                  full(k1.shape), full(k2.shape)],
        out_specs=(pl.BlockSpec((D, tt), lambda i: (0, i)), head_out, head_out, head_out, head_out),
        out_shape=(jax.ShapeDtypeStruct((D, N), jnp.bfloat16), row_shape, row_shape, half_shape, half_shape),
        compiler_params=pltpu.CompilerParams(
            dimension_semantics=("arbitrary",), vmem_limit_bytes=VMEM_LIMIT),
        name="peer_prep",
    )(h1, gain, wqt, k1, k2)


def _peer_ffn_kernel(u_ref, vt_ref, hft_ref, cnt_ref, e1_ref, r2_ref, e2_ref, h_ref, gain_ref,
                     out_ref, acc, *, eb):
    j = pl.program_id(1)
    NH, NK = PEER_HEADS, PEER_N_KEYS
    tt = hft_ref.shape[1]
    bf = jnp.bfloat16

    @pl.when(j == 0)
    def _():
        acc[...] = jnp.zeros(acc.shape, jnp.float32)

    def row_bcast(ref, h, a):
        return jnp.broadcast_to(ref[h, a:a + 1, :], (NK, tt)).astype(bf)

    def activations(sb):
        return _dot(u_ref[sb * PEER_SUB:(sb + 1) * PEER_SUB, :], hft_ref[...])

    def gates(sb):
        blocks = []
        for a in range(sb * (PEER_SUB // NK), (sb + 1) * (PEER_SUB // NK)):
            g = None
            for h in range(NH):
                hit = r2_ref[h] < row_bcast(cnt_ref, h, a)
                term = jnp.where(hit, e2_ref[h] * row_bcast(e1_ref, h, a), jnp.zeros((), bf))
                g = term if g is None else g + term
            blocks.append(g)
        return jnp.concatenate(blocks, axis=0)

    n_sub = eb // PEER_SUB
    act_cur, g_cur = activations(0), gates(0)
    for sb in range(n_sub):
        r0 = sb * PEER_SUB
        if sb + 1 < n_sub:
            act_next = activations(sb + 1)
        gelu2 = act_cur * (1.0 + lax.erf(act_cur * np.float32(math.sqrt(0.5))))
        acc[...] += _dot(vt_ref[:, r0:r0 + PEER_SUB], g_cur * gelu2.astype(bf))
        if sb + 1 < n_sub:
            act_cur, g_cur = act_next, gates(sb + 1)

    @pl.when(j == pl.num_programs(1) - 1)
    def _():
        h2 = h_ref[...] + acc[...].T
        out_ref[...] = _rms_rows(h2, gain_ref[...])


def _peer_ffn(u_bf, vt_bf, hft, cnt, e1, r2, e2, h1, gain, *, tt, eb):
    N, D = h1.shape
    E = u_bf.shape[0]
    NH, NK = PEER_HEADS, PEER_N_KEYS
    ab = eb // NK
    return pl.pallas_call(
        functools.partial(_peer_ffn_kernel, eb=eb),
        grid=(N // tt, E // eb),
        in_specs=[pl.BlockSpec((eb, D), lambda i, j: (j, 0)),
                  pl.BlockSpec((D, eb), lambda i, j: (0, j)),
                  pl.BlockSpec((D, tt), lambda i, j: (0, i)),
                  pl.BlockSpec((NH, ab, tt), lambda i, j: (0, j, i)),
                  pl.BlockSpec((NH, ab, tt), lambda i, j: (0, j, i)),
                  pl.BlockSpec((NH, NK, tt), lambda i, j: (0, 0, i)),
                  pl.BlockSpec((NH, NK, tt), lambda i, j: (0, 0, i)),
                  pl.BlockSpec((tt, D), lambda i, j: (i, 0)),
                  pl.BlockSpec(gain.shape, lambda i, j: (0, 0))],
        out_specs=pl.BlockSpec((tt, D), lambda i, j: (i, 0)),
        out_shape=jax.ShapeDtypeStruct((N, D), jnp.float32),
        scratch_shapes=[pltpu.VMEM((D, tt), jnp.float32)],
        compiler_params=pltpu.CompilerParams(
            dimension_semantics=("arbitrary", "arbitrary"), vmem_limit_bytes=VMEM_LIMIT),
        name="peer_ffn",
    )(u_bf, vt_bf, hft, cnt, e1, r2, e2, h1, gain)


def _tiles(B, S):
    ts_in = min(S, 512)
    ts_gdn = min(S, 512)
    tt = min(B * S, 512)
    eb = 2048
    return ts_in, ts_gdn, tt, eb


def kernel(x, norm_mix_gain, w_in, conv_w, a_log, dt_bias, gdn_out_gain, w_pool, b_pool, pool_scale,
           w_out, norm_ffn_gain, peer_w_query, peer_sub_keys_1, peer_sub_keys_2, peer_expert_u,
           peer_expert_v, norm_final_gain):
    B, S, D = x.shape
    depth = w_in.shape[0]
    pw = pool_scale.shape[1]
    gw = GDN_HEADS * GDN_HEAD_DIM
    H = GDN_HEADS
    ts_in, ts_gdn, tt, eb = _tiles(B, S)
    assert S % ts_in == 0 and S % ts_gdn == 0 and ts_gdn % CHUNK == 0 and (B * S) % tt == 0
    bf = jnp.bfloat16
    sel, selr = _gdn_constants()

    h = x
    for l in range(depth):
        wl = w_in[l]
        wp = wl[:, 0:pw].astype(bf)
        wqkv = wl[:, pw:pw + 3 * gw].astype(bf)
        wz = wl[:, pw + 3 * gw:pw + 4 * gw].astype(bf)
        wba = jnp.pad(wl[:, pw + 4 * gw:pw + 4 * gw + 2 * H], ((0, 0), (0, LANES - 2 * H))).astype(bf)
        alog_pad = jnp.pad(a_log[l], (H, LANES - 2 * H))[None, :]
        dtb_pad = jnp.pad(dt_bias[l], (H, LANES - 2 * H))[None, :]
        pool, q, k, v, gate, bg = _mix_in(
            h, norm_mix_gain[l][None, :], wp, wqkv, wz, wba, conv_w[l], alog_pad, dtb_pad,
            w_pool[l].astype(bf), b_pool[l].reshape(1, pw), pool_scale[l][None, :], ts=ts_in)
        h1 = _gdn_out(h, pool, q, k, v, gate, bg, jnp.tile(gdn_out_gain[l], H)[None, :],
                      w_out[l].astype(bf), sel, selr, ts=ts_gdn)

        h1f = h1.reshape(B * S, D)
        hft, cnt, e1, r2, e2 = _peer_prep(
            h1f, norm_ffn_gain[l][None, :], peer_w_query[l].T.astype(bf),
            peer_sub_keys_1[l], peer_sub_keys_2[l], tt=tt)
        assert depth == 1
        out = _peer_ffn(peer_expert_u[l].astype(bf), peer_expert_v[l].T.astype(bf), hft, cnt, e1, r2, e2,
                        h1f, norm_final_gain[None, :], tt=tt, eb=eb)
        h = out.reshape(B, S, D)
    return h
```

```python
import functools
import math

import jax
import jax.numpy as jnp
import numpy as np
from jax import lax
from jax.experimental import pallas as pl
from jax.experimental.pallas import tpu as pltpu

NORM_EPS = 1e-6
CHUNK = 64
POOL_WINDOWS = (2, 4, 8, 16)
CONV_WIDTH = 4
GDN_HEADS = 4
GDN_HEAD_DIM = 128
PEER_HEADS = 8
PEER_N_KEYS = 128
PEER_HALF = 128
PEER_TOPK = 16
PEER_SUB = 256
PEER_AHEAD = 8

LANES = 128
SUBLANES = 8
POOL_CARRY = 16
CONV_CARRY = 8
VMEM_LIMIT = 56 * 1024 * 1024


def _dot(a, b):
    return jnp.dot(a, b, preferred_element_type=jnp.float32)


def _dot_nt(a, b):
    return lax.dot_general(a, b, (((1,), (1,)), ((), ())), preferred_element_type=jnp.float32)


def _dot_tn(a, b):
    return lax.dot_general(a, b, (((0,), (0,)), ((), ())), preferred_element_type=jnp.float32)


def _rms_rows(x, gain):
    return x * lax.rsqrt(jnp.mean(x * x, axis=-1, keepdims=True) + NORM_EPS) * gain


def _mix_in_kernel(x_ref, gain_ref, wp_ref, wqkv_ref, wz_ref, wba_ref, convw_ref, alog_ref, dtb_ref,
                   wpool_ref, bpool_ref, pscale_ref,
                   pool_ref, q_ref, k_ref, v_ref, gate_ref, bg_ref,
                   pbuf, cbuf, *, ts):
    s = pl.program_id(1)

    @pl.when(s == 0)
    def _():
        pbuf[0:POOL_CARRY, :] = jnp.zeros((POOL_CARRY, pbuf.shape[1]), jnp.float32)
        cbuf[0:CONV_CARRY, :] = jnp.zeros((CONV_CARRY, cbuf.shape[1]), jnp.float32)

    @pl.when(s > 0)
    def _():
        pbuf[0:POOL_CARRY, :] = pbuf[ts:ts + POOL_CARRY, :]
        cbuf[0:CONV_CARRY, :] = cbuf[ts:ts + CONV_CARRY, :]

    x = x_ref[0]
    hn = _rms_rows(x, gain_ref[...]).astype(jnp.bfloat16)

    p = _dot(hn, wp_ref[...])
    pbuf[POOL_CARRY:POOL_CARRY + ts, :] = p
    pos1 = (lax.broadcasted_iota(jnp.int32, (ts, LANES), 0) + s * ts + 1).astype(jnp.float32)
    for g, win in enumerate(POOL_WINDOWS):
        cols = slice(g * LANES, (g + 1) * LANES)
        ws = pbuf[POOL_CARRY:POOL_CARRY + ts, cols]
        for j in range(1, win):
            ws = ws + pbuf[POOL_CARRY - j:POOL_CARRY - j + ts, cols]
        d = ws / jnp.minimum(pos1, float(win)) - p[:, cols]
        y = _dot(d.astype(jnp.bfloat16), wpool_ref[g]) + bpool_ref[:, cols]
        pool_ref[0, :, cols] = y * pscale_ref[:, cols]

    cbuf[CONV_CARRY:CONV_CARRY + ts, :] = _dot(hn, wqkv_ref[...])
    acc = convw_ref[0:1, :] * cbuf[CONV_CARRY - 3:CONV_CARRY - 3 + ts, :]
    for j in range(1, CONV_WIDTH):
        acc = acc + convw_ref[j:j + 1, :] * cbuf[CONV_CARRY - 3 + j:CONV_CARRY - 3 + j + ts, :]
    qkv = acc * jax.nn.sigmoid(acc)
    width = GDN_HEADS * GDN_HEAD_DIM
    for h in range(GDN_HEADS):
        cq = slice(h * GDN_HEAD_DIM, (h + 1) * GDN_HEAD_DIM)
        ck = slice(width + h * GDN_HEAD_DIM, width + (h + 1) * GDN_HEAD_DIM)
        cv = slice(2 * width + h * GDN_HEAD_DIM, 2 * width + (h + 1) * GDN_HEAD_DIM)
        qh = qkv[:, cq]
        kh = qkv[:, ck]
        q_ref[0, :, cq] = qh * lax.rsqrt(jnp.sum(qh * qh, axis=-1, keepdims=True) + NORM_EPS) * (GDN_HEAD_DIM ** -0.5)
        k_ref[0, :, cq] = kh * lax.rsqrt(jnp.sum(kh * kh, axis=-1, keepdims=True) + NORM_EPS)
        v_ref[0, :, cq] = qkv[:, cv]

    z = _dot(hn, wz_ref[...])
    gate_ref[0] = z * jax.nn.sigmoid(z)

    ba = _dot(hn, wba_ref[...])
    lane = lax.broadcasted_iota(jnp.int32, ba.shape, 1)
    sp_in = ba + dtb_ref[...]
    softplus = jnp.maximum(sp_in, 0.0) + jnp.log1p(jnp.exp(-jnp.abs(sp_in)))
    gdec = -jnp.exp(alog_ref[...]) * softplus
    bg_ref[0] = jnp.where(lane < GDN_HEADS, jax.nn.sigmoid(ba),
                          jnp.where(lane < 2 * GDN_HEADS, gdec, 0.0))


def _mix_in(x, gain, wp, wqkv, wz, wba, convw, alog_pad, dtb_pad, wpool, bpool, pscale, *, ts):
    B, S, D = x.shape
    pw = wp.shape[1]
    gw = wz.shape[1]
    full = lambda shape: pl.BlockSpec(shape, lambda b, s: (0,) * len(shape))
    tile = lambda w: pl.BlockSpec((1, ts, w), lambda b, s: (b, s, 0))
    out_shapes = (
        jax.ShapeDtypeStruct((B, S, pw), jnp.float32),
        jax.ShapeDtypeStruct((B, S, gw), jnp.float32),
        jax.ShapeDtypeStruct((B, S, gw), jnp.float32),
        jax.ShapeDtypeStruct((B, S, gw), jnp.float32),
        jax.ShapeDtypeStruct((B, S, gw), jnp.float32),
        jax.ShapeDtypeStruct((B, S, LANES), jnp.float32),
    )
    return pl.pallas_call(
        functools.partial(_mix_in_kernel, ts=ts),
        grid=(B, S // ts),
        in_specs=[tile(D), full(gain.shape), full(wp.shape), full(wqkv.shape), full(wz.shape),
                  full(wba.shape), full(convw.shape), full(alog_pad.shape), full(dtb_pad.shape),
                  full(wpool.shape), full(bpool.shape), full(pscale.shape)],
        out_specs=(tile(pw), tile(gw), tile(gw), tile(gw), tile(gw), tile(LANES)),
        out_shape=out_shapes,
        scratch_shapes=[pltpu.VMEM((POOL_CARRY + ts, pw), jnp.float32),
                        pltpu.VMEM((CONV_CARRY + ts, wqkv.shape[1]), jnp.float32)],
        compiler_params=pltpu.CompilerParams(
            dimension_semantics=("arbitrary", "arbitrary"), vmem_limit_bytes=VMEM_LIMIT),
        name="mix_in",
    )(x, gain, wp, wqkv, wz, wba, convw, alog_pad, dtb_pad, wpool, bpool, pscale)


def _split3(a):
    p1 = a.astype(jnp.bfloat16)
    r1 = a - p1.astype(jnp.float32)
    p2 = r1.astype(jnp.bfloat16)
    p3 = (r1 - p2.astype(jnp.float32)).astype(jnp.bfloat16)
    return p1, p2, p3


def _gdn_out_kernel(x_ref, pool_ref, q_ref, k_ref, v_ref, gate_ref, bg_ref, ogain_ref, wout_ref,
                    sel_ref, selr_ref,
                    h_ref, state, o_buf, *, ts):
    s = pl.program_id(1)
    H, Dh, C = GDN_HEADS, GDN_HEAD_DIM, CHUNK
    R = H * C
    bf = jnp.bfloat16

    @pl.when(s == 0)
    def _():
        state[...] = jnp.zeros(state.shape, jnp.float32)

    ri = lax.broadcasted_iota(jnp.int32, (R, R), 0)
    ci = lax.broadcasted_iota(jnp.int32, (R, R), 1)
    same = (ri // C) == (ci // C)
    causal = same & (ri >= ci)
    strict = same & (ri > ci)
    eye = (ri == ci).astype(jnp.float32)

    bg = bg_ref[0]
    trow = lax.broadcasted_iota(jnp.int32, (ts, LANES), 0) % C
    tlane = lax.broadcasted_iota(jnp.int32, (ts, LANES), 1)
    cum = bg
    step = 1
    while step < C:
        cum = cum + jnp.where(trow >= step, pltpu.roll(cum, step, axis=0), 0.0)
        step *= 2
    tot = jnp.concatenate(
        [jnp.broadcast_to(cum[c * C + C - 1:c * C + C, :], (C, LANES)) for c in range(ts // C)], axis=0)
    feat = jnp.where(tlane < H, bg,
                     jnp.where(tlane < 2 * H, cum,
                               jnp.where(tlane < 3 * H, pltpu.roll(tot, H, axis=1), 0.0)))
    lane = lax.broadcasted_iota(jnp.int32, (C, LANES), 1)

    n_chunks = ts // C
    stacked = lambda ref, rows: jnp.concatenate([ref[0, rows, h * Dh:(h + 1) * Dh] for h in range(H)], axis=0)

    pre = []
    for c in range(n_chunks):
        rows = slice(c * C, (c + 1) * C)
        fc = feat[rows]
        fm = jnp.concatenate([jnp.where(lane % H == h, fc, 0.0) for h in range(H)], axis=0)
        fcat = jnp.concatenate(_split3(fm), axis=1)
        bc = _dot(fcat, sel_ref[...])
        gcum_cols = _dot_nt(selr_ref[...], fcat)
        pre.append(dict(rows=rows, bc=bc, gcum_cols=gcum_cols))
    for p in pre:
        bc = p["bc"]
        p["gcum"] = bc[:, 0:Dh]
        p["gtot"] = bc[:, R + Dh:R + 2 * Dh]
        diff = bc[:, 0:R] - p["gcum_cols"]
        p["dec"] = jnp.where(causal, jnp.exp(jnp.where(causal, diff, 0.0)), 0.0)
        beta_b = bc[:, R:R + Dh]
        ks = stacked(k_ref, p["rows"])
        p["ks"] = ks
        p["ks_bf"] = ks.astype(bf)
        p["kb"] = ks * beta_b
        p["vb"] = stacked(v_ref, p["rows"]) * beta_b
        kk = _dot_nt(p["kb"].astype(bf), p["ks_bf"])
        m = jnp.where(strict, -(kk * p["dec"]), 0.0)
        p["tinv"] = eye + m
        p["pw"] = m
    for _ in range(int(math.log2(C)) - 1):
        for p in pre:
            pw_bf = p["pw"].astype(bf)
            p["pw"] = _dot(pw_bf, pw_bf)
        for p in pre:
            p["tinv"] = p["tinv"] + _dot(p["pw"].astype(bf), p["tinv"].astype(bf))
    for p in pre:
        egc = jnp.exp(p["gcum"])
        sol = _dot(p["tinv"].astype(bf), jnp.concatenate([p["vb"], p["kb"] * egc], axis=1).astype(bf))
        qs = stacked(q_ref, p["rows"])
        p["u"] = sol[:, 0:Dh]
        p["w"] = sol[:, Dh:2 * Dh].astype(bf)
        p["qk"] = (_dot_nt(qs.astype(bf), p["ks_bf"]) * p["dec"]).astype(bf)
        p["qd"] = (qs * egc).astype(bf)
        p["kd"] = (p["ks"] * jnp.exp(p["gtot"] - p["gcum"])).astype(bf)
        p["cd"] = jnp.exp(p["gtot"])

    st = [state[h] for h in range(H)]
    for p in pre:
        st_bf = [t.astype(bf) for t in st]
        u, w, qd, kd, cd = p["u"], p["w"], p["qd"], p["kd"], p["cd"]
        vnew = jnp.concatenate(
            [u[h * C:(h + 1) * C] - _dot(w[h * C:(h + 1) * C], st_bf[h]) for h in range(H)], axis=0)
        vnew_bf = vnew.astype(bf)
        o_intra = _dot(p["qk"], vnew_bf)
        for h in range(H):
            hr = slice(h * C, (h + 1) * C)
            o_buf[p["rows"], h * Dh:(h + 1) * Dh] = _dot(qd[hr], st_bf[h]) + o_intra[hr]
            st[h] = st[h] * cd[h * C:h * C + 1, :] + _dot_tn(kd[hr], vnew_bf[hr])
    for h in range(H):
        state[h] = st[h]

    o = o_buf[...]
    gdn = jnp.concatenate(
        [o[:, h * Dh:(h + 1) * Dh]
         * lax.rsqrt(jnp.mean(o[:, h * Dh:(h + 1) * Dh] * o[:, h * Dh:(h + 1) * Dh], axis=-1, keepdims=True) + NORM_EPS)
         for h in range(H)], axis=1)
    gdn = gdn * ogain_ref[...] * gate_ref[0]
    mixed = jnp.concatenate([pool_ref[0], gdn], axis=1).astype(jnp.bfloat16)
    h_ref[0] = x_ref[0] + _dot(mixed, wout_ref[...])


def _gdn_out(x, pool, q, k, v, gate, bg, ogain, wout, sel, selr, *, ts):
    B, S, D = x.shape
    gw = q.shape[2]
    full = lambda shape: pl.BlockSpec(shape, lambda b, s: (0,) * len(shape))
    tile = lambda w: pl.BlockSpec((1, ts, w), lambda b, s: (b, s, 0))
    return pl.pallas_call(
        functools.partial(_gdn_out_kernel, ts=ts),
        grid=(B, S // ts),
        in_specs=[tile(D), tile(pool.shape[2]), tile(gw), tile(gw), tile(gw), tile(gw), tile(LANES),
                  full(ogain.shape), full(wout.shape), full(sel.shape), full(selr.shape)],
        out_specs=tile(D),
        out_shape=jax.ShapeDtypeStruct((B, S, D), jnp.float32),
        scratch_shapes=[pltpu.VMEM((GDN_HEADS, GDN_HEAD_DIM, GDN_HEAD_DIM), jnp.float32),
                        pltpu.VMEM((ts, gw), jnp.float32)],
        compiler_params=pltpu.CompilerParams(
            dimension_semantics=("arbitrary", "arbitrary"), vmem_limit_bytes=VMEM_LIMIT),
        name="gdn_out",
    )(x, pool, q, k, v, gate, bg, ogain, wout, sel, selr)


def _gdn_constants():
    H, C, Dh = GDN_HEADS, CHUNK, GDN_HEAD_DIM
    R = H * C
    sel = np.zeros((3 * LANES, R + 2 * Dh), np.float32)
    selr = np.zeros((R, 3 * LANES), np.float32)
    for piece in range(3):
        base = piece * LANES
        sel[base + H:base + 2 * H, 0:R] = 1.0
        sel[base:base + H, R:R + Dh] = 1.0
        sel[base + 2 * H:base + 3 * H, R + Dh:R + 2 * Dh] = 1.0
        selr[:, base + H:base + 2 * H] = 1.0
    return jnp.asarray(sel, jnp.bfloat16), jnp.asarray(selr, jnp.bfloat16)


def _sorting_network(n):
    def merge(lo, hi, r):
        step = r * 2
        if step < hi - lo:
            yield from merge(lo, hi, step)
            yield from merge(lo + r, hi, step)
            yield from ((i, i + r) for i in range(lo + r, hi - r, step))
        else:
            yield (lo, lo + r)

    def sort(lo, hi):
        if hi - lo >= 1:
            mid = lo + (hi - lo) // 2
            yield from sort(lo, mid)
            yield from sort(mid + 1, hi)
            yield from merge(lo, hi, 1)

    return tuple(sort(0, n - 1))


def _top_values(x, n):
    lists = [x[i * SUBLANES:(i + 1) * SUBLANES] for i in range(x.shape[0] // SUBLANES)]
    for i, j in _sorting_network(len(lists)):
        lists[i], lists[j] = jnp.maximum(lists[i], lists[j]), jnp.minimum(lists[i], lists[j])
    tops = []
    for r in range(n):
        m = jnp.max(lists[0], axis=0, keepdims=True)
        tops.append(m)
        depth = min(n - 1 - r, len(lists))
        if depth:
            hit = lists[0] == m
            for i in range(depth):
                nxt = lists[i + 1] if i + 1 < len(lists) else -jnp.inf
                lists[i] = jnp.where(hit, nxt, lists[i])
    return tops


def _ranks(x, tops):
    rank = jnp.zeros(x.shape, jnp.float32)
    for r, m in enumerate(tops):
        rank = jnp.where(x < m, float(r + 1), rank)
    return rank


def _peer_prep_kernel(h_ref, gain_ref, wqt_ref, k1_ref, k2_ref,
                      hft_ref, cnt_ref, e1_ref, r2_ref, e2_ref):
    NH, K = PEER_HEADS, PEER_TOPK
    hf = _rms_rows(h_ref[...], gain_ref[...])
    hft = hf.T.astype(jnp.bfloat16)
    hft_ref[...] = hft
    qt = _dot(wqt_ref[...], hft)

    s1, s2, a_top, b_top = [], [], [], []
    for h in range(NH):
        base = h * 2 * PEER_HALF
        s1h = _dot(k1_ref[h], qt[base:base + PEER_HALF])
        s2h = _dot(k2_ref[h], qt[base + PEER_HALF:base + 2 * PEER_HALF])
        s1.append(s1h)
        s2.append(s2h)
        a_top.append(_top_values(s1h, K))
        b_top.append(_top_values(s2h, K))
        r2_ref[h] = _ranks(s2h, b_top[h]).astype(jnp.bfloat16)

    A = [jnp.concatenate([a_top[h][i] for h in range(NH)], axis=0) for i in range(K)]
    Bv = [jnp.concatenate([b_top[h][j] for h in range(NH)], axis=0) for j in range(K)]
    cells = [(i, j) for i in range(K) for j in range(K) if (i + 1) * (j + 1) <= K]
    cand = {(i, j): A[i] + Bv[j] for (i, j) in cells}
    work = list(cand.values())
    for r in range(K):
        m = work[0]
        for cnd in work[1:]:
            m = jnp.maximum(m, cnd)
        if r + 1 < K:
            work = [jnp.where(cnd == m, -jnp.inf, cnd) for cnd in work]
    tau = m
    top = A[0] + Bv[0]
    z = jnp.zeros_like(tau)
    counts = [jnp.zeros_like(tau) for _ in range(K)]
    for (i, j), cnd in cand.items():
        hit = cnd >= tau
        z = z + jnp.where(hit, jnp.exp(cnd - top), 0.0)
        counts[i] = counts[i] + jnp.where(hit, 1.0, 0.0)
    zinv = 0.5 / z

    floors = []
    for c in range(1, K // 2 + 1):
        f = jnp.full_like(tau, jnp.inf)
        for i in range(1, K):
            f = jnp.minimum(f, jnp.where(counts[i] >= float(c), A[i], jnp.inf))
        floors.append(f)

    for h in range(NH):
        cnt = jnp.zeros_like(s1[h])
        for c, f in enumerate(floors):
            cnt = jnp.where(s1[h] >= f[h:h + 1, :], float(c + 1), cnt)
        cnt = jnp.where(s1[h] >= A[0][h:h + 1, :], counts[0][h:h + 1, :], cnt)
        cnt_ref[h] = cnt
        e1_ref[h] = jnp.exp(s1[h] - A[0][h:h + 1, :]) * zinv[h:h + 1, :]
        e2_ref[h] = jnp.exp(s2[h] - Bv[0][h:h + 1, :]).astype(jnp.bfloat16)


def _peer_prep(h1, gain, wqt, k1, k2, *, tt):
    N, D = h1.shape
    NH, NK = k1.shape[0], k1.shape[1]
    full = lambda shape: pl.BlockSpec(shape, lambda i: (0,) * len(shape))
    head_out = pl.BlockSpec((NH, NK, tt), lambda i: (0, 0, i))
    row_shape = jax.ShapeDtypeStruct((NH, NK, N), jnp.float32)
    half_shape = jax.ShapeDtypeStruct((NH, NK, N), jnp.bfloat16)
    return pl.pallas_call(
        _peer_prep_kernel,
        grid=(N // tt,),
        in_specs=[pl.BlockSpec((tt, D), lambda i: (i, 0)), full(gain.shape), full(wqt.shape),
                  full(k1.shape), full(k2.shape)],
        out_specs=(pl.BlockSpec((D, tt), lambda i: (0, i)), head_out, head_out, head_out, head_out),
        out_shape=(jax.ShapeDtypeStruct((D, N), jnp.bfloat16), row_shape, row_shape, half_shape, half_shape),
        compiler_params=pltpu.CompilerParams(
            dimension_semantics=("arbitrary",), vmem_limit_bytes=VMEM_LIMIT),
        name="peer_prep",
    )(h1, gain, wqt, k1, k2)


def _peer_ffn_kernel(u_ref, vt_ref, hft_ref, cnt_ref, e1_ref, r2_ref, e2_ref, h_ref, gain_ref,
                     out_ref, acc, *, eb):
    j = pl.program_id(1)
    NH, NK = PEER_HEADS, PEER_N_KEYS
    tt = hft_ref.shape[1]
    bf = jnp.bfloat16

    @pl.when(j == 0)
    def _():
        acc[...] = jnp.zeros(acc.shape, jnp.float32)

    def row_bcast(ref, h, a):
        return jnp.broadcast_to(ref[h, a:a + 1, :], (NK, tt)).astype(bf)

    def activations(sb):
        return _dot(u_ref[sb * PEER_SUB:(sb + 1) * PEER_SUB, :], hft_ref[...])

    def gates(sb):
        blocks = []
        for a in range(sb * (PEER_SUB // NK), (sb + 1) * (PEER_SUB // NK)):
            g = None
            for h in range(NH):
                hit = r2_ref[h] < row_bcast(cnt_ref, h, a)
                term = jnp.where(hit, e2_ref[h] * row_bcast(e1_ref, h, a), jnp.zeros((), bf))
                g = term if g is None else g + term
            blocks.append(g)
        return jnp.concatenate(blocks, axis=0)

    n_sub = eb // PEER_SUB
    acts = {sb: activations(sb) for sb in range(min(PEER_AHEAD, n_sub))}
    g_cur = gates(0)
    for sb in range(n_sub):
        r0 = sb * PEER_SUB
        if sb + PEER_AHEAD < n_sub:
            acts[sb + PEER_AHEAD] = activations(sb + PEER_AHEAD)
        act = acts.pop(sb)
        gelu2 = act * (1.0 + lax.erf(act * np.float32(math.sqrt(0.5))))
        acc[...] += _dot(vt_ref[:, r0:r0 + PEER_SUB], g_cur * gelu2.astype(bf))
        if sb + 1 < n_sub:
            g_cur = gates(sb + 1)

    @pl.when(j == pl.num_programs(1) - 1)
    def _():
        h2 = h_ref[...] + acc[...].T
        out_ref[...] = _rms_rows(h2, gain_ref[...])


def _peer_ffn(u_bf, vt_bf, hft, cnt, e1, r2, e2, h1, gain, *, tt, eb):
    N, D = h1.shape
    E = u_bf.shape[0]
    NH, NK = PEER_HEADS, PEER_N_KEYS
    ab = eb // NK
    return pl.pallas_call(
        functools.partial(_peer_ffn_kernel, eb=eb),
        grid=(N // tt, E // eb),
        in_specs=[pl.BlockSpec((eb, D), lambda i, j: (j, 0)),
                  pl.BlockSpec((D, eb), lambda i, j: (0, j)),
                  pl.BlockSpec((D, tt), lambda i, j: (0, i)),
                  pl.BlockSpec((NH, ab, tt), lambda i, j: (0, j, i)),
                  pl.BlockSpec((NH, ab, tt), lambda i, j: (0, j, i)),
                  pl.BlockSpec((NH, NK, tt), lambda i, j: (0, 0, i)),
                  pl.BlockSpec((NH, NK, tt), lambda i, j: (0, 0, i)),
                  pl.BlockSpec((tt, D), lambda i, j: (i, 0)),
                  pl.BlockSpec(gain.shape, lambda i, j: (0, 0))],
        out_specs=pl.BlockSpec((tt, D), lambda i, j: (i, 0)),
        out_shape=jax.ShapeDtypeStruct((N, D), jnp.float32),
        scratch_shapes=[pltpu.VMEM((D, tt), jnp.float32)],
        compiler_params=pltpu.CompilerParams(
            dimension_semantics=("arbitrary", "arbitrary"), vmem_limit_bytes=VMEM_LIMIT),
        name="peer_ffn",
    )(u_bf, vt_bf, hft, cnt, e1, r2, e2, h1, gain)


def _tiles(B, S):
    ts_in = min(S, 512)
    ts_gdn = min(S, 512)
    tt = min(B * S, 512)
    eb = 2048
    return ts_in, ts_gdn, tt, eb


def kernel(x, norm_mix_gain, w_in, conv_w, a_log, dt_bias, gdn_out_gain, w_pool, b_pool, pool_scale,
           w_out, norm_ffn_gain, peer_w_query, peer_sub_keys_1, peer_sub_keys_2, peer_expert_u,
           peer_expert_v, norm_final_gain):
    B, S, D = x.shape
    depth = w_in.shape[0]
    pw = pool_scale.shape[1]
    gw = GDN_HEADS * GDN_HEAD_DIM
    H = GDN_HEADS
    ts_in, ts_gdn, tt, eb = _tiles(B, S)
    assert S % ts_in == 0 and S % ts_gdn == 0 and ts_gdn % CHUNK == 0 and (B * S) % tt == 0
    bf = jnp.bfloat16
    sel, selr = _gdn_constants()

    h = x
    for l in range(depth):
        wl = w_in[l]
        wp = wl[:, 0:pw].astype(bf)
        wqkv = wl[:, pw:pw + 3 * gw].astype(bf)
        wz = wl[:, pw + 3 * gw:pw + 4 * gw].astype(bf)
        wba = jnp.pad(wl[:, pw + 4 * gw:pw + 4 * gw + 2 * H], ((0, 0), (0, LANES - 2 * H))).astype(bf)
        alog_pad = jnp.pad(a_log[l], (H, LANES - 2 * H))[None, :]
        dtb_pad = jnp.pad(dt_bias[l], (H, LANES - 2 * H))[None, :]
        pool, q, k, v, gate, bg = _mix_in(
            h, norm_mix_gain[l][None, :], wp, wqkv, wz, wba, conv_w[l], alog_pad, dtb_pad,
            w_pool[l].astype(bf), b_pool[l].reshape(1, pw), pool_scale[l][None, :], ts=ts_in)
        h1 = _gdn_out(h, pool, q, k, v, gate, bg, jnp.tile(gdn_out_gain[l], H)[None, :],
                      w_out[l].astype(bf), sel, selr, ts=ts_gdn)

        h1f = h1.reshape(B * S, D)
        hft, cnt, e1, r2, e2 = _peer_prep(
            h1f, norm_ffn_gain[l][None, :], peer_w_query[l].T.astype(bf),
            peer_sub_keys_1[l], peer_sub_keys_2[l], tt=tt)
        assert depth == 1
        out = _peer_ffn(peer_expert_u[l].astype(bf), peer_expert_v[l].T.astype(bf), hft, cnt, e1, r2, e2,
                        h1f, norm_final_gain[None, :], tt=tt, eb=eb)
        h = out.reshape(B, S, D)
    return h
```

```python
import functools
import math

import jax
import jax.numpy as jnp
import numpy as np
from jax import lax
from jax.experimental import pallas as pl
from jax.experimental.pallas import tpu as pltpu

NORM_EPS = 1e-6
CHUNK = 64
POOL_WINDOWS = (2, 4, 8, 16)
CONV_WIDTH = 4
GDN_HEADS = 4
GDN_HEAD_DIM = 128
PEER_HEADS = 8
PEER_N_KEYS = 128
PEER_HALF = 128
PEER_TOPK = 16
PEER_SUB = 256
PEER_AHEAD = 16

LANES = 128
SUBLANES = 8
POOL_CARRY = 16
CONV_CARRY = 8
VMEM_LIMIT = 62 * 1024 * 1024


def _dot(a, b):
    return jnp.dot(a, b, preferred_element_type=jnp.float32)


def _dot_nt(a, b):
    return lax.dot_general(a, b, (((1,), (1,)), ((), ())), preferred_element_type=jnp.float32)


def _dot_tn(a, b):
    return lax.dot_general(a, b, (((0,), (0,)), ((), ())), preferred_element_type=jnp.float32)


def _rms_rows(x, gain):
    return x * lax.rsqrt(jnp.mean(x * x, axis=-1, keepdims=True) + NORM_EPS) * gain


def _mix_in_kernel(x_ref, gain_ref, wp_ref, wqkv_ref, wz_ref, wba_ref, convw_ref, alog_ref, dtb_ref,
                   wpool_ref, bpool_ref, pscale_ref,
                   pool_ref, q_ref, k_ref, v_ref, gate_ref, bg_ref,
                   pbuf, cbuf, *, ts):
    s = pl.program_id(1)

    @pl.when(s == 0)
    def _():
        pbuf[0:POOL_CARRY, :] = jnp.zeros((POOL_CARRY, pbuf.shape[1]), jnp.float32)
        cbuf[0:CONV_CARRY, :] = jnp.zeros((CONV_CARRY, cbuf.shape[1]), jnp.float32)

    @pl.when(s > 0)
    def _():
        pbuf[0:POOL_CARRY, :] = pbuf[ts:ts + POOL_CARRY, :]
        cbuf[0:CONV_CARRY, :] = cbuf[ts:ts + CONV_CARRY, :]

    x = x_ref[0]
    hn = _rms_rows(x, gain_ref[...]).astype(jnp.bfloat16)

    p = _dot(hn, wp_ref[...])
    pbuf[POOL_CARRY:POOL_CARRY + ts, :] = p
    pos1 = (lax.broadcasted_iota(jnp.int32, (ts, LANES), 0) + s * ts + 1).astype(jnp.float32)
    for g, win in enumerate(POOL_WINDOWS):
        cols = slice(g * LANES, (g + 1) * LANES)
        ext = pbuf[0:POOL_CARRY + ts, cols]
        span = 1
        while span < win:
            ext = ext + pltpu.roll(ext, span, axis=0)
            span *= 2
        ws = ext[POOL_CARRY:POOL_CARRY + ts]
        d = ws / jnp.minimum(pos1, float(win)) - p[:, cols]
        y = _dot(d.astype(jnp.bfloat16), wpool_ref[g]) + bpool_ref[:, cols]
        pool_ref[0, :, cols] = y * pscale_ref[:, cols]

    cbuf[CONV_CARRY:CONV_CARRY + ts, :] = _dot(hn, wqkv_ref[...])
    acc = convw_ref[0:1, :] * cbuf[CONV_CARRY - 3:CONV_CARRY - 3 + ts, :]
    for j in range(1, CONV_WIDTH):
        acc = acc + convw_ref[j:j + 1, :] * cbuf[CONV_CARRY - 3 + j:CONV_CARRY - 3 + j + ts, :]
    qkv = acc * jax.nn.sigmoid(acc)
    width = GDN_HEADS * GDN_HEAD_DIM
    for h in range(GDN_HEADS):
        cq = slice(h * GDN_HEAD_DIM, (h + 1) * GDN_HEAD_DIM)
        ck = slice(width + h * GDN_HEAD_DIM, width + (h + 1) * GDN_HEAD_DIM)
        cv = slice(2 * width + h * GDN_HEAD_DIM, 2 * width + (h + 1) * GDN_HEAD_DIM)
        qh = qkv[:, cq]
        kh = qkv[:, ck]
        q_ref[0, :, cq] = qh * lax.rsqrt(jnp.sum(qh * qh, axis=-1, keepdims=True) + NORM_EPS) * (GDN_HEAD_DIM ** -0.5)
        k_ref[0, :, cq] = kh * lax.rsqrt(jnp.sum(kh * kh, axis=-1, keepdims=True) + NORM_EPS)
        v_ref[0, :, cq] = qkv[:, cv]

    z = _dot(hn, wz_ref[...])
    gate_ref[0] = z * jax.nn.sigmoid(z)

    ba = _dot(hn, wba_ref[...])
    lane = lax.broadcasted_iota(jnp.int32, ba.shape, 1)
    sp_in = ba + dtb_ref[...]
    softplus = jnp.maximum(sp_in, 0.0) + jnp.log1p(jnp.exp(-jnp.abs(sp_in)))
    gdec = -jnp.exp(alog_ref[...]) * softplus
    bg_ref[0] = jnp.where(lane < GDN_HEADS, jax.nn.sigmoid(ba),
                          jnp.where(lane < 2 * GDN_HEADS, gdec, 0.0))


def _mix_in(x, gain, wp, wqkv, wz, wba, convw, alog_pad, dtb_pad, wpool, bpool, pscale, *, ts):
    B, S, D = x.shape
    pw = wp.shape[1]
    gw = wz.shape[1]
    full = lambda shape: pl.BlockSpec(shape, lambda b, s: (0,) * len(shape))
    tile = lambda w: pl.BlockSpec((1, ts, w), lambda b, s: (b, s, 0))
    out_shapes = (
        jax.ShapeDtypeStruct((B, S, pw), jnp.float32),
        jax.ShapeDtypeStruct((B, S, gw), jnp.float32),
        jax.ShapeDtypeStruct((B, S, gw), jnp.float32),
        jax.ShapeDtypeStruct((B, S, gw), jnp.float32),
        jax.ShapeDtypeStruct((B, S, gw), jnp.float32),
        jax.ShapeDtypeStruct((B, S, LANES), jnp.float32),
    )
    return pl.pallas_call(
        functools.partial(_mix_in_kernel, ts=ts),
        grid=(B, S // ts),
        in_specs=[tile(D), full(gain.shape), full(wp.shape), full(wqkv.shape), full(wz.shape),
                  full(wba.shape), full(convw.shape), full(alog_pad.shape), full(dtb_pad.shape),
                  full(wpool.shape), full(bpool.shape), full(pscale.shape)],
        out_specs=(tile(pw), tile(gw), tile(gw), tile(gw), tile(gw), tile(LANES)),
        out_shape=out_shapes,
        scratch_shapes=[pltpu.VMEM((POOL_CARRY + ts, pw), jnp.float32),
                        pltpu.VMEM((CONV_CARRY + ts, wqkv.shape[1]), jnp.float32)],
        compiler_params=pltpu.CompilerParams(
            dimension_semantics=("arbitrary", "arbitrary"), vmem_limit_bytes=VMEM_LIMIT),
        name="mix_in",
    )(x, gain, wp, wqkv, wz, wba, convw, alog_pad, dtb_pad, wpool, bpool, pscale)


def _split3(a):
    p1 = a.astype(jnp.bfloat16)
    r1 = a - p1.astype(jnp.float32)
    p2 = r1.astype(jnp.bfloat16)
    p3 = (r1 - p2.astype(jnp.float32)).astype(jnp.bfloat16)
    return p1, p2, p3


def _gdn_out_kernel(x_ref, pool_ref, q_ref, k_ref, v_ref, gate_ref, bg_ref, ogain_ref, wout_ref,
                    sel_ref, selr_ref,
                    h_ref, state, o_buf, *, ts):
    s = pl.program_id(1)
    H, Dh, C = GDN_HEADS, GDN_HEAD_DIM, CHUNK
    R = H * C
    bf = jnp.bfloat16

    @pl.when(s == 0)
    def _():
        state[...] = jnp.zeros(state.shape, jnp.float32)

    ri = lax.broadcasted_iota(jnp.int32, (R, R), 0)
    ci = lax.broadcasted_iota(jnp.int32, (R, R), 1)
    same = (ri // C) == (ci // C)
    causal = same & (ri >= ci)
    strict = same & (ri > ci)
    eye = (ri == ci).astype(jnp.float32)

    bg = bg_ref[0]
    trow = lax.broadcasted_iota(jnp.int32, (ts, LANES), 0) % C
    tlane = lax.broadcasted_iota(jnp.int32, (ts, LANES), 1)
    cum = bg
    step = 1
    while step < C:
        cum = cum + jnp.where(trow >= step, pltpu.roll(cum, step, axis=0), 0.0)
        step *= 2
    tot = jnp.concatenate(
        [jnp.broadcast_to(cum[c * C + C - 1:c * C + C, :], (C, LANES)) for c in range(ts // C)], axis=0)
    feat = jnp.where(tlane < H, bg,
                     jnp.where(tlane < 2 * H, cum,
                               jnp.where(tlane < 3 * H, pltpu.roll(tot, H, axis=1), 0.0)))
    lane = lax.broadcasted_iota(jnp.int32, (C, LANES), 1)

    n_chunks = ts // C
    stacked = lambda ref, rows: jnp.concatenate([ref[0, rows, h * Dh:(h + 1) * Dh] for h in range(H)], axis=0)

    pre = []
    for c in range(n_chunks):
        rows = slice(c * C, (c + 1) * C)
        fc = feat[rows]
        fm = jnp.concatenate([jnp.where(lane % H == h, fc, 0.0) for h in range(H)], axis=0)
        fcat = jnp.concatenate(_split3(fm), axis=1)
        bc = _dot(fcat, sel_ref[...])
        gcum_cols = _dot_nt(selr_ref[...], fcat)
        pre.append(dict(rows=rows, bc=bc, gcum_cols=gcum_cols))
    for p in pre:
        bc = p["bc"]
        p["gcum"] = bc[:, 0:Dh]
        p["gtot"] = bc[:, R + Dh:R + 2 * Dh]
        diff = bc[:, 0:R] - p["gcum_cols"]
        p["dec"] = jnp.where(causal, jnp.exp(jnp.where(causal, diff, 0.0)), 0.0)
        beta_b = bc[:, R:R + Dh]
        ks = stacked(k_ref, p["rows"])
        p["ks"] = ks
        p["ks_bf"] = ks.astype(bf)
        p["kb"] = ks * beta_b
        p["vb"] = stacked(v_ref, p["rows"]) * beta_b
        kk = _dot_nt(p["kb"].astype(bf), p["ks_bf"])
        m = jnp.where(strict, -(kk * p["dec"]), 0.0)
        p["tinv"] = eye + m
        p["pw"] = m
    for _ in range(int(math.log2(C)) - 1):
        for p in pre:
            pw_bf = p["pw"].astype(bf)
            p["pw"] = _dot(pw_bf, pw_bf)
        for p in pre:
            p["tinv"] = p["tinv"] + _dot(p["pw"].astype(bf), p["tinv"].astype(bf))
    for p in pre:
        egc = jnp.exp(p["gcum"])
        sol = _dot(p["tinv"].astype(bf), jnp.concatenate([p["vb"], p["kb"] * egc], axis=1).astype(bf))
        qs = stacked(q_ref, p["rows"])
        p["u"] = sol[:, 0:Dh]
        p["w"] = sol[:, Dh:2 * Dh].astype(bf)
        p["qk"] = (_dot_nt(qs.astype(bf), p["ks_bf"]) * p["dec"]).astype(bf)
        p["qd"] = (qs * egc).astype(bf)
        p["kd"] = (p["ks"] * jnp.exp(p["gtot"] - p["gcum"])).astype(bf)
        p["cd"] = jnp.exp(p["gtot"])

    st = [state[h] for h in range(H)]
    for p in pre:
        st_bf = [t.astype(bf) for t in st]
        u, w, qd, kd, cd = p["u"], p["w"], p["qd"], p["kd"], p["cd"]
        vnew = jnp.concatenate(
            [u[h * C:(h + 1) * C] - _dot(w[h * C:(h + 1) * C], st_bf[h]) for h in range(H)], axis=0)
        vnew_bf = vnew.astype(bf)
        o_intra = _dot(p["qk"], vnew_bf)
        for h in range(H):
            hr = slice(h * C, (h + 1) * C)
            o_buf[p["rows"], h * Dh:(h + 1) * Dh] = _dot(qd[hr], st_bf[h]) + o_intra[hr]
            st[h] = st[h] * cd[h * C:h * C + 1, :] + _dot_tn(kd[hr], vnew_bf[hr])
    for h in range(H):
        state[h] = st[h]

    o = o_buf[...]
    gdn = jnp.concatenate(
        [o[:, h * Dh:(h + 1) * Dh]
         * lax.rsqrt(jnp.mean(o[:, h * Dh:(h + 1) * Dh] * o[:, h * Dh:(h + 1) * Dh], axis=-1, keepdims=True) + NORM_EPS)
         for h in range(H)], axis=1)
    gdn = gdn * ogain_ref[...] * gate_ref[0]
    mixed = jnp.concatenate([pool_ref[0], gdn], axis=1).astype(jnp.bfloat16)
    h_ref[0] = x_ref[0] + _dot(mixed, wout_ref[...])


def _gdn_out(x, pool, q, k, v, gate, bg, ogain, wout, sel, selr, *, ts):
    B, S, D = x.shape
    gw = q.shape[2]
    full = lambda shape: pl.BlockSpec(shape, lambda b, s: (0,) * len(shape))
    tile = lambda w: pl.BlockSpec((1, ts, w), lambda b, s: (b, s, 0))
    return pl.pallas_call(
        functools.partial(_gdn_out_kernel, ts=ts),
        grid=(B, S // ts),
        in_specs=[tile(D), tile(pool.shape[2]), tile(gw), tile(gw), tile(gw), tile(gw), tile(LANES),
                  full(ogain.shape), full(wout.shape), full(sel.shape), full(selr.shape)],
        out_specs=tile(D),
        out_shape=jax.ShapeDtypeStruct((B, S, D), jnp.float32),
        scratch_shapes=[pltpu.VMEM((GDN_HEADS, GDN_HEAD_DIM, GDN_HEAD_DIM), jnp.float32),
                        pltpu.VMEM((ts, gw), jnp.float32)],
        compiler_params=pltpu.CompilerParams(
            dimension_semantics=("arbitrary", "arbitrary"), vmem_limit_bytes=VMEM_LIMIT),
        name="gdn_out",
    )(x, pool, q, k, v, gate, bg, ogain, wout, sel, selr)


def _gdn_constants():
    H, C, Dh = GDN_HEADS, CHUNK, GDN_HEAD_DIM
    R = H * C
    sel = np.zeros((3 * LANES, R + 2 * Dh), np.float32)
    selr = np.zeros((R, 3 * LANES), np.float32)
    for piece in range(3):
        base = piece * LANES
        sel[base + H:base + 2 * H, 0:R] = 1.0
        sel[base:base + H, R:R + Dh] = 1.0
        sel[base + 2 * H:base + 3 * H, R + Dh:R + 2 * Dh] = 1.0
        selr[:, base + H:base + 2 * H] = 1.0
    return jnp.asarray(sel, jnp.bfloat16), jnp.asarray(selr, jnp.bfloat16)


def _sorting_network(n):
    def merge(lo, hi, r):
        step = r * 2
        if step < hi - lo:
            yield from merge(lo, hi, step)
            yield from merge(lo + r, hi, step)
            yield from ((i, i + r) for i in range(lo + r, hi - r, step))
        else:
            yield (lo, lo + r)

    def sort(lo, hi):
        if hi - lo >= 1:
            mid = lo + (hi - lo) // 2
            yield from sort(lo, mid)
            yield from sort(mid + 1, hi)
            yield from merge(lo, hi, 1)

    return tuple(sort(0, n - 1))


def _top_values(x, n):
    lists = [x[i * SUBLANES:(i + 1) * SUBLANES] for i in range(x.shape[0] // SUBLANES)]
    for i, j in _sorting_network(len(lists)):
        lists[i], lists[j] = jnp.maximum(lists[i], lists[j]), jnp.minimum(lists[i], lists[j])
    tops = []
    for r in range(n):
        m = jnp.max(lists[0], axis=0, keepdims=True)
        tops.append(m)
        depth = min(n - 1 - r, len(lists))
        if depth:
            hit = lists[0] == m
            for i in range(depth):
                nxt = lists[i + 1] if i + 1 < len(lists) else -jnp.inf
                lists[i] = jnp.where(hit, nxt, lists[i])
    return tops


def _ranks(x, tops):
    rank = jnp.zeros(x.shape, jnp.float32)
    for r, m in enumerate(tops):
        rank = jnp.where(x < m, float(r + 1), rank)
    return rank


def _peer_prep_kernel(h_ref, gain_ref, wqt_ref, k1_ref, k2_ref,
                      hft_ref, cnt_ref, e1_ref, r2_ref, e2_ref):
    NH, K = PEER_HEADS, PEER_TOPK
    hf = _rms_rows(h_ref[...], gain_ref[...])
    hft = hf.T.astype(jnp.bfloat16)
    hft_ref[...] = hft
    qt = _dot(wqt_ref[...], hft)

    s1, s2, a_top, b_top = [], [], [], []
    for h in range(NH):
        base = h * 2 * PEER_HALF
        s1h = _dot(k1_ref[h], qt[base:base + PEER_HALF])
        s2h = _dot(k2_ref[h], qt[base + PEER_HALF:base + 2 * PEER_HALF])
        s1.append(s1h)
        s2.append(s2h)
        a_top.append(_top_values(s1h, K))
        b_top.append(_top_values(s2h, K))
        r2_ref[h] = _ranks(s2h, b_top[h]).astype(jnp.bfloat16)

    A = [jnp.concatenate([a_top[h][i] for h in range(NH)], axis=0) for i in range(K)]
    Bv = [jnp.concatenate([b_top[h][j] for h in range(NH)], axis=0) for j in range(K)]
    cells = [(i, j) for i in range(K) for j in range(K) if (i + 1) * (j + 1) <= K]
    cand = {(i, j): A[i] + Bv[j] for (i, j) in cells}
    work = list(cand.values())
    for r in range(K):
        m = work[0]
        for cnd in work[1:]:
            m = jnp.maximum(m, cnd)
        if r + 1 < K:
            work = [jnp.where(cnd == m, -jnp.inf, cnd) for cnd in work]
    tau = m
    top = A[0] + Bv[0]
    z = jnp.zeros_like(tau)
    counts = [jnp.zeros_like(tau) for _ in range(K)]
    for (i, j), cnd in cand.items():
        hit = cnd >= tau
        z = z + jnp.where(hit, jnp.exp(cnd - top), 0.0)
        counts[i] = counts[i] + jnp.where(hit, 1.0, 0.0)
    zinv = 0.5 / z

    floors = []
    for c in range(1, K // 2 + 1):
        f = jnp.full_like(tau, jnp.inf)
        for i in range(1, K):
            f = jnp.minimum(f, jnp.where(counts[i] >= float(c), A[i], jnp.inf))
        floors.append(f)

    for h in range(NH):
        cnt = jnp.zeros_like(s1[h])
        for c, f in enumerate(floors):
            cnt = jnp.where(s1[h] >= f[h:h + 1, :], float(c + 1), cnt)
        cnt = jnp.where(s1[h] >= A[0][h:h + 1, :], counts[0][h:h + 1, :], cnt)
        cnt_ref[h] = cnt
        e1_ref[h] = jnp.exp(s1[h] - A[0][h:h + 1, :]) * zinv[h:h + 1, :]
        e2_ref[h] = jnp.exp(s2[h] - Bv[0][h:h + 1, :]).astype(jnp.bfloat16)


def _peer_prep(h1, gain, wqt, k1, k2, *, tt):
    N, D = h1.shape
    NH, NK = k1.shape[0], k1.shape[1]
    full = lambda shape: pl.BlockSpec(shape, lambda i: (0,) * len(shape))
    head_out = pl.BlockSpec((NH, NK, tt), lambda i: (0, 0, i))
    row_shape = jax.ShapeDtypeStruct((NH, NK, N), jnp.float32)
    half_shape = jax.ShapeDtypeStruct((NH, NK, N), jnp.bfloat16)
    return pl.pallas_call(
        _peer_prep_kernel,
        grid=(N // tt,),
        in_specs=[pl.BlockSpec((tt, D), lambda i: (i, 0)), full(gain.shape), full(wqt.shape),
                  full(k1.shape), full(k2.shape)],
        out_specs=(pl.BlockSpec((D, tt), lambda i: (0, i)), head_out, head_out, head_out, head_out),
        out_shape=(jax.ShapeDtypeStruct((D, N), jnp.bfloat16), row_shape, row_shape, half_shape, half_shape),
        compiler_params=pltpu.CompilerParams(
            dimension_semantics=("arbitrary",), vmem_limit_bytes=VMEM_LIMIT),
        name="peer_prep",
    )(h1, gain, wqt, k1, k2)


def _peer_ffn_kernel(u_ref, vt_ref, hft_ref, cnt_ref, e1_ref, r2_ref, e2_ref, h_ref, gain_ref,
                     out_ref, acc, *, eb):
    j = pl.program_id(1)
    NH, NK = PEER_HEADS, PEER_N_KEYS
    tt = hft_ref.shape[1]
    bf = jnp.bfloat16

    @pl.when(j == 0)
    def _():
        acc[...] = jnp.zeros(acc.shape, jnp.float32)

    def row_bcast(ref, h, a):
        return jnp.broadcast_to(ref[h, a:a + 1, :], (NK, tt)).astype(bf)

    def activations(sb):
        return _dot(u_ref[sb * PEER_SUB:(sb + 1) * PEER_SUB, :], hft_ref[...])

    def gates(sb):
        blocks = []
        for a in range(sb * (PEER_SUB // NK), (sb + 1) * (PEER_SUB // NK)):
            g = None
            for h in range(NH):
                hit = r2_ref[h] < row_bcast(cnt_ref, h, a)
                term = jnp.where(hit, e2_ref[h] * row_bcast(e1_ref, h, a), jnp.zeros((), bf))
                g = term if g is None else g + term
            blocks.append(g)
        return jnp.concatenate(blocks, axis=0)

    n_sub = eb // PEER_SUB
    acts = {sb: activations(sb) for sb in range(min(PEER_AHEAD, n_sub))}
    g_cur = gates(0)
    for sb in range(n_sub):
        r0 = sb * PEER_SUB
        if sb + PEER_AHEAD < n_sub:
            acts[sb + PEER_AHEAD] = activations(sb + PEER_AHEAD)
        act = acts.pop(sb)
        act = act.astype(bf)
        gelu2 = act * (1.0 + lax.erf(act * jnp.asarray(math.sqrt(0.5), bf)))
        acc[...] += _dot(vt_ref[:, r0:r0 + PEER_SUB], g_cur * gelu2)
        if sb + 1 < n_sub:
            g_cur = gates(sb + 1)

    @pl.when(j == pl.num_programs(1) - 1)
    def _():
        h2 = h_ref[...] + acc[...].T
        out_ref[...] = _rms_rows(h2, gain_ref[...])


def _peer_ffn(u_bf, vt_bf, hft, cnt, e1, r2, e2, h1, gain, *, tt, eb):
    N, D = h1.shape
    E = u_bf.shape[0]
    NH, NK = PEER_HEADS, PEER_N_KEYS
    ab = eb // NK
    return pl.pallas_call(
        functools.partial(_peer_ffn_kernel, eb=eb),
        grid=(N // tt, E // eb),
        in_specs=[pl.BlockSpec((eb, D), lambda i, j: (j, 0)),
                  pl.BlockSpec((D, eb), lambda i, j: (0, j)),
                  pl.BlockSpec((D, tt), lambda i, j: (0, i)),
                  pl.BlockSpec((NH, ab, tt), lambda i, j: (0, j, i)),
                  pl.BlockSpec((NH, ab, tt), lambda i, j: (0, j, i)),
                  pl.BlockSpec((NH, NK, tt), lambda i, j: (0, 0, i)),
                  pl.BlockSpec((NH, NK, tt), lambda i, j: (0, 0, i)),
                  pl.BlockSpec((tt, D), lambda i, j: (i, 0)),
                  pl.BlockSpec(gain.shape, lambda i, j: (0, 0))],
        out_specs=pl.BlockSpec((tt, D), lambda i, j: (i, 0)),
        out_shape=jax.ShapeDtypeStruct((N, D), jnp.float32),
        scratch_shapes=[pltpu.VMEM((D, tt), jnp.float32)],
        compiler_params=pltpu.CompilerParams(
            dimension_semantics=("arbitrary", "arbitrary"), vmem_limit_bytes=VMEM_LIMIT),
        name="peer_ffn",
    )(u_bf, vt_bf, hft, cnt, e1, r2, e2, h1, gain)


def _tiles(B, S):
    ts_in = min(S, 512)
    ts_gdn = min(S, 512)
    tt = min(B * S, 512)
    eb = 4096
    return ts_in, ts_gdn, tt, eb


def kernel(x, norm_mix_gain, w_in, conv_w, a_log, dt_bias, gdn_out_gain, w_pool, b_pool, pool_scale,
           w_out, norm_ffn_gain, peer_w_query, peer_sub_keys_1, peer_sub_keys_2, peer_expert_u,
           peer_expert_v, norm_final_gain):
    B, S, D = x.shape
    depth = w_in.shape[0]
    pw = pool_scale.shape[1]
    gw = GDN_HEADS * GDN_HEAD_DIM
    H = GDN_HEADS
    ts_in, ts_gdn, tt, eb = _tiles(B, S)
    assert S % ts_in == 0 and S % ts_gdn == 0 and ts_gdn % CHUNK == 0 and (B * S) % tt == 0
    bf = jnp.bfloat16
    sel, selr = _gdn_constants()

    h = x
    for l in range(depth):
        wl = w_in[l]
        wp = wl[:, 0:pw].astype(bf)
        wqkv = wl[:, pw:pw + 3 * gw].astype(bf)
        wz = wl[:, pw + 3 * gw:pw + 4 * gw].astype(bf)
        wba = jnp.pad(wl[:, pw + 4 * gw:pw + 4 * gw + 2 * H], ((0, 0), (0, LANES - 2 * H))).astype(bf)
        alog_pad = jnp.pad(a_log[l], (H, LANES - 2 * H))[None, :]
        dtb_pad = jnp.pad(dt_bias[l], (H, LANES - 2 * H))[None, :]
        pool, q, k, v, gate, bg = _mix_in(
            h, norm_mix_gain[l][None, :], wp, wqkv, wz, wba, conv_w[l], alog_pad, dtb_pad,
            w_pool[l].astype(bf), b_pool[l].reshape(1, pw), pool_scale[l][None, :], ts=ts_in)
        h1 = _gdn_out(h, pool, q, k, v, gate, bg, jnp.tile(gdn_out_gain[l], H)[None, :],
                      w_out[l].astype(bf), sel, selr, ts=ts_gdn)

        h1f = h1.reshape(B * S, D)
        hft, cnt, e1, r2, e2 = _peer_prep(
            h1f, norm_ffn_gain[l][None, :], peer_w_query[l].T.astype(bf),
            peer_sub_keys_1[l], peer_sub_keys_2[l], tt=tt)
        assert depth == 1
        out = _peer_ffn(peer_expert_u[l].astype(bf), peer_expert_v[l].T.astype(bf), hft, cnt, e1, r2, e2,
                        h1f, norm_final_gain[None, :], tt=tt, eb=eb)
        h = out.reshape(B, S, D)
    return h
```

```python
import functools
import math

import jax
import jax.numpy as jnp
import numpy as np
from jax import lax
from jax.experimental import pallas as pl
from jax.experimental.pallas import tpu as pltpu

NORM_EPS = 1e-6
CHUNK = 64
POOL_WINDOWS = (2, 4, 8, 16)
CONV_WIDTH = 4
GDN_HEADS = 4
GDN_HEAD_DIM = 128
PEER_HEADS = 8
PEER_N_KEYS = 128
PEER_HALF = 128
PEER_TOPK = 16
PEER_SUB = 256
PEER_AHEAD = 16

LANES = 128
SUBLANES = 8
POOL_CARRY = 16
CONV_CARRY = 8
VMEM_LIMIT = 62 * 1024 * 1024


def _dot(a, b):
    return jnp.dot(a, b, preferred_element_type=jnp.float32)


def _dot_nt(a, b):
    return lax.dot_general(a, b, (((1,), (1,)), ((), ())), preferred_element_type=jnp.float32)


def _dot_tn(a, b):
    return lax.dot_general(a, b, (((0,), (0,)), ((), ())), preferred_element_type=jnp.float32)


def _rms_rows(x, gain):
    return x * lax.rsqrt(jnp.mean(x * x, axis=-1, keepdims=True) + NORM_EPS) * gain


def _mix_in_kernel(x_ref, gain_ref, wp_ref, wqkv_ref, wz_ref, wba_ref, convw_ref, alog_ref, dtb_ref,
                   wpool_ref, bpool_ref, pscale_ref,
                   pool_ref, q_ref, k_ref, v_ref, gate_ref, bg_ref,
                   pbuf, cbuf, *, ts):
    s = pl.program_id(1)

    @pl.when(s == 0)
    def _():
        pbuf[0:POOL_CARRY, :] = jnp.zeros((POOL_CARRY, pbuf.shape[1]), jnp.float32)
        cbuf[0:CONV_CARRY, :] = jnp.zeros((CONV_CARRY, cbuf.shape[1]), jnp.float32)

    @pl.when(s > 0)
    def _():
        pbuf[0:POOL_CARRY, :] = pbuf[ts:ts + POOL_CARRY, :]
        cbuf[0:CONV_CARRY, :] = cbuf[ts:ts + CONV_CARRY, :]

    x = x_ref[0]
    hn = _rms_rows(x, gain_ref[...]).astype(jnp.bfloat16)

    p = _dot(hn, wp_ref[...])
    pbuf[POOL_CARRY:POOL_CARRY + ts, :] = p
    pos1 = (lax.broadcasted_iota(jnp.int32, (ts, LANES), 0) + s * ts + 1).astype(jnp.float32)
    for g, win in enumerate(POOL_WINDOWS):
        cols = slice(g * LANES, (g + 1) * LANES)
        ext = pbuf[0:POOL_CARRY + ts, cols]
        span = 1
        while span < win:
            ext = ext + pltpu.roll(ext, span, axis=0)
            span *= 2
        ws = ext[POOL_CARRY:POOL_CARRY + ts]
        d = ws / jnp.minimum(pos1, float(win)) - p[:, cols]
        y = _dot(d.astype(jnp.bfloat16), wpool_ref[g]) + bpool_ref[:, cols]
        pool_ref[0, :, cols] = y * pscale_ref[:, cols]

    cbuf[CONV_CARRY:CONV_CARRY + ts, :] = _dot(hn, wqkv_ref[...])
    acc = convw_ref[0:1, :] * cbuf[CONV_CARRY - 3:CONV_CARRY - 3 + ts, :]
    for j in range(1, CONV_WIDTH):
        acc = acc + convw_ref[j:j + 1, :] * cbuf[CONV_CARRY - 3 + j:CONV_CARRY - 3 + j + ts, :]
    qkv = acc * jax.nn.sigmoid(acc)
    width = GDN_HEADS * GDN_HEAD_DIM
    for h in range(GDN_HEADS):
        cq = slice(h * GDN_HEAD_DIM, (h + 1) * GDN_HEAD_DIM)
        ck = slice(width + h * GDN_HEAD_DIM, width + (h + 1) * GDN_HEAD_DIM)
        cv = slice(2 * width + h * GDN_HEAD_DIM, 2 * width + (h + 1) * GDN_HEAD_DIM)
        qh = qkv[:, cq]
        kh = qkv[:, ck]
        q_ref[0, :, cq] = qh * lax.rsqrt(jnp.sum(qh * qh, axis=-1, keepdims=True) + NORM_EPS) * (GDN_HEAD_DIM ** -0.5)
        k_ref[0, :, cq] = kh * lax.rsqrt(jnp.sum(kh * kh, axis=-1, keepdims=True) + NORM_EPS)
        v_ref[0, :, cq] = qkv[:, cv]

    z = _dot(hn, wz_ref[...])
    gate_ref[0] = z * jax.nn.sigmoid(z)

    ba = _dot(hn, wba_ref[...])
    lane = lax.broadcasted_iota(jnp.int32, ba.shape, 1)
    sp_in = ba + dtb_ref[...]
    softplus = jnp.maximum(sp_in, 0.0) + jnp.log1p(jnp.exp(-jnp.abs(sp_in)))
    gdec = -jnp.exp(alog_ref[...]) * softplus
    bg_ref[0] = jnp.where(lane < GDN_HEADS, jax.nn.sigmoid(ba),
                          jnp.where(lane < 2 * GDN_HEADS, gdec, 0.0))


def _mix_in(x, gain, wp, wqkv, wz, wba, convw, alog_pad, dtb_pad, wpool, bpool, pscale, *, ts):
    B, S, D = x.shape
    pw = wp.shape[1]
    gw = wz.shape[1]
    full = lambda shape: pl.BlockSpec(shape, lambda b, s: (0,) * len(shape))
    tile = lambda w: pl.BlockSpec((1, ts, w), lambda b, s: (b, s, 0))
    out_shapes = (
        jax.ShapeDtypeStruct((B, S, pw), jnp.float32),
        jax.ShapeDtypeStruct((B, S, gw), jnp.float32),
        jax.ShapeDtypeStruct((B, S, gw), jnp.float32),
        jax.ShapeDtypeStruct((B, S, gw), jnp.float32),
        jax.ShapeDtypeStruct((B, S, gw), jnp.float32),
        jax.ShapeDtypeStruct((B, S, LANES), jnp.float32),
    )
    return pl.pallas_call(
        functools.partial(_mix_in_kernel, ts=ts),
        grid=(B, S // ts),
        in_specs=[tile(D), full(gain.shape), full(wp.shape), full(wqkv.shape), full(wz.shape),
                  full(wba.shape), full(convw.shape), full(alog_pad.shape), full(dtb_pad.shape),
                  full(wpool.shape), full(bpool.shape), full(pscale.shape)],
        out_specs=(tile(pw), tile(gw), tile(gw), tile(gw), tile(gw), tile(LANES)),
        out_shape=out_shapes,
        scratch_shapes=[pltpu.VMEM((POOL_CARRY + ts, pw), jnp.float32),
                        pltpu.VMEM((CONV_CARRY + ts, wqkv.shape[1]), jnp.float32)],
        compiler_params=pltpu.CompilerParams(
            dimension_semantics=("arbitrary", "arbitrary"), vmem_limit_bytes=VMEM_LIMIT),
        name="mix_in",
    )(x, gain, wp, wqkv, wz, wba, convw, alog_pad, dtb_pad, wpool, bpool, pscale)


def _split3(a):
    p1 = a.astype(jnp.bfloat16)
    r1 = a - p1.astype(jnp.float32)
    p2 = r1.astype(jnp.bfloat16)
    p3 = (r1 - p2.astype(jnp.float32)).astype(jnp.bfloat16)
    return p1, p2, p3


def _gdn_out_kernel(x_ref, pool_ref, q_ref, k_ref, v_ref, gate_ref, bg_ref, ogain_ref, wout_ref,
                    sel_ref,
                    h_ref, state, o_buf, *, ts):
    s = pl.program_id(1)
    H, Dh, C = GDN_HEADS, GDN_HEAD_DIM, CHUNK
    R = H * C
    bf = jnp.bfloat16

    @pl.when(s == 0)
    def _():
        state[...] = jnp.zeros(state.shape, jnp.float32)

    ri = lax.broadcasted_iota(jnp.int32, (R, R), 0)
    ci = lax.broadcasted_iota(jnp.int32, (R, R), 1)
    same = (ri // C) == (ci // C)
    causal = same & (ri >= ci)
    strict = same & (ri > ci)
    eye = (ri == ci).astype(jnp.float32)

    bg = bg_ref[0]
    trow = lax.broadcasted_iota(jnp.int32, (ts, LANES), 0) % C
    tlane = lax.broadcasted_iota(jnp.int32, (ts, LANES), 1)
    cum = bg
    step = 1
    while step < C:
        cum = cum + jnp.where(trow >= step, pltpu.roll(cum, step, axis=0), 0.0)
        step *= 2
    tot = jnp.concatenate(
        [jnp.broadcast_to(cum[c * C + C - 1:c * C + C, :], (C, LANES)) for c in range(ts // C)], axis=0)
    feat = jnp.where(tlane < H, bg,
                     jnp.where(tlane < 2 * H, cum,
                               jnp.where(tlane < 3 * H, pltpu.roll(tot, H, axis=1), 0.0)))
    lane = lax.broadcasted_iota(jnp.int32, (C, LANES), 1)

    n_chunks = ts // C
    stacked = lambda ref, rows: jnp.concatenate([ref[0, rows, h * Dh:(h + 1) * Dh] for h in range(H)], axis=0)

    pre = []
    for c in range(n_chunks):
        rows = slice(c * C, (c + 1) * C)
        fc = feat[rows]
        fm = jnp.concatenate([jnp.where(lane % H == h, fc, 0.0) for h in range(H)], axis=0)
        fcat = jnp.concatenate(_split3(fm), axis=1)
        bc = _dot(fcat, sel_ref[...])
        gcum_cols = bc[:, 0:R].T
        pre.append(dict(rows=rows, bc=bc, gcum_cols=gcum_cols))
    for p in pre:
        bc = p["bc"]
        p["gcum"] = bc[:, 0:Dh]
        p["gtot"] = bc[:, R + Dh:R + 2 * Dh]
        diff = bc[:, 0:R] - p["gcum_cols"]
        p["dec"] = jnp.where(causal, jnp.exp(jnp.where(causal, diff, 0.0)), 0.0)
        beta_b = bc[:, R:R + Dh]
        ks = stacked(k_ref, p["rows"])
        p["ks"] = ks
        p["ks_bf"] = ks.astype(bf)
        p["kb"] = ks * beta_b
        p["vb"] = stacked(v_ref, p["rows"]) * beta_b
        kk = _dot_nt(p["kb"].astype(bf), p["ks_bf"])
        m = jnp.where(strict, -(kk * p["dec"]), 0.0)
        p["tinv"] = eye + m
        p["pw"] = m
    for _ in range(int(math.log2(C)) - 1):
        for p in pre:
            pw_bf = p["pw"].astype(bf)
            p["pw"] = _dot(pw_bf, pw_bf)
        for p in pre:
            p["tinv"] = p["tinv"] + _dot(p["pw"].astype(bf), p["tinv"].astype(bf))
    for p in pre:
        egc = jnp.exp(p["gcum"])
        sol = _dot(p["tinv"].astype(bf), jnp.concatenate([p["vb"], p["kb"] * egc], axis=1).astype(bf))
        qs = stacked(q_ref, p["rows"])
        p["u"] = sol[:, 0:Dh]
        p["w"] = sol[:, Dh:2 * Dh].astype(bf)
        p["qk"] = (_dot_nt(qs.astype(bf), p["ks_bf"]) * p["dec"]).astype(bf)
        p["qd"] = (qs * egc).astype(bf)
        p["kd"] = (p["ks"] * jnp.exp(p["gtot"] - p["gcum"])).astype(bf)
        p["cd"] = jnp.exp(p["gtot"])

    st = [state[h] for h in range(H)]
    for p in pre:
        st_bf = [t.astype(bf) for t in st]
        u, w, qd, kd, cd = p["u"], p["w"], p["qd"], p["kd"], p["cd"]
        vnew = jnp.concatenate(
            [u[h * C:(h + 1) * C] - _dot(w[h * C:(h + 1) * C], st_bf[h]) for h in range(H)], axis=0)
        vnew_bf = vnew.astype(bf)
        o_intra = _dot(p["qk"], vnew_bf)
        for h in range(H):
            hr = slice(h * C, (h + 1) * C)
            o_buf[p["rows"], h * Dh:(h + 1) * Dh] = _dot(qd[hr], st_bf[h]) + o_intra[hr]
            st[h] = st[h] * cd[h * C:h * C + 1, :] + _dot_tn(kd[hr], vnew_bf[hr])
    for h in range(H):
        state[h] = st[h]

    o = o_buf[...]
    gdn = jnp.concatenate(
        [o[:, h * Dh:(h + 1) * Dh]
         * lax.rsqrt(jnp.mean(o[:, h * Dh:(h + 1) * Dh] * o[:, h * Dh:(h + 1) * Dh], axis=-1, keepdims=True) + NORM_EPS)
         for h in range(H)], axis=1)
    gdn = gdn * ogain_ref[...] * gate_ref[0]
    mixed = jnp.concatenate([pool_ref[0], gdn], axis=1).astype(jnp.bfloat16)
    h_ref[0] = x_ref[0] + _dot(mixed, wout_ref[...])


def _gdn_out(x, pool, q, k, v, gate, bg, ogain, wout, sel, *, ts):
    B, S, D = x.shape
    gw = q.shape[2]
    full = lambda shape: pl.BlockSpec(shape, lambda b, s: (0,) * len(shape))
    tile = lambda w: pl.BlockSpec((1, ts, w), lambda b, s: (b, s, 0))
    return pl.pallas_call(
        functools.partial(_gdn_out_kernel, ts=ts),
        grid=(B, S // ts),
        in_specs=[tile(D), tile(pool.shape[2]), tile(gw), tile(gw), tile(gw), tile(gw), tile(LANES),
                  full(ogain.shape), full(wout.shape), full(sel.shape)],
        out_specs=tile(D),
        out_shape=jax.ShapeDtypeStruct((B, S, D), jnp.float32),
        scratch_shapes=[pltpu.VMEM((GDN_HEADS, GDN_HEAD_DIM, GDN_HEAD_DIM), jnp.float32),
                        pltpu.VMEM((ts, gw), jnp.float32)],
        compiler_params=pltpu.CompilerParams(
            dimension_semantics=("arbitrary", "arbitrary"), vmem_limit_bytes=VMEM_LIMIT),
        name="gdn_out",
    )(x, pool, q, k, v, gate, bg, ogain, wout, sel)


def _gdn_constants():
    H, C, Dh = GDN_HEADS, CHUNK, GDN_HEAD_DIM
    R = H * C
    sel = np.zeros((3 * LANES, R + 2 * Dh), np.float32)
    for piece in range(3):
        base = piece * LANES
        sel[base + H:base + 2 * H, 0:R] = 1.0
        sel[base:base + H, R:R + Dh] = 1.0
        sel[base + 2 * H:base + 3 * H, R + Dh:R + 2 * Dh] = 1.0
    return jnp.asarray(sel, jnp.bfloat16)


def _sorting_network(n):
    def merge(lo, hi, r):
        step = r * 2
        if step < hi - lo:
            yield from merge(lo, hi, step)
            yield from merge(lo + r, hi, step)
            yield from ((i, i + r) for i in range(lo + r, hi - r, step))
        else:
            yield (lo, lo + r)

    def sort(lo, hi):
        if hi - lo >= 1:
            mid = lo + (hi - lo) // 2
            yield from sort(lo, mid)
            yield from sort(mid + 1, hi)
            yield from merge(lo, hi, 1)

    return tuple(sort(0, n - 1))


def _top_values(x, n):
    lists = [x[i * SUBLANES:(i + 1) * SUBLANES] for i in range(x.shape[0] // SUBLANES)]
    for i, j in _sorting_network(len(lists)):
        lists[i], lists[j] = jnp.maximum(lists[i], lists[j]), jnp.minimum(lists[i], lists[j])
    tops = []
    for r in range(n):
        m = jnp.max(lists[0], axis=0, keepdims=True)
        tops.append(m)
        depth = min(n - 1 - r, len(lists))
        if depth:
            hit = lists[0] == m
            for i in range(depth):
                nxt = lists[i + 1] if i + 1 < len(lists) else -jnp.inf
                lists[i] = jnp.where(hit, nxt, lists[i])
    return tops


def _ranks(x, tops):
    rank = jnp.zeros(x.shape, jnp.float32)
    for r, m in enumerate(tops):
        rank = jnp.where(x < m, float(r + 1), rank)
    return rank


def _peer_prep_kernel(h_ref, gain_ref, wqt_ref, k1_ref, k2_ref,
                      hft_ref, cnt_ref, e1_ref, r2_ref, e2_ref):
    NH, K = PEER_HEADS, PEER_TOPK
    hf = _rms_rows(h_ref[...], gain_ref[...])
    hft = hf.T.astype(jnp.bfloat16)
    hft_ref[...] = hft
    qt = _dot(wqt_ref[...], hft)

    s1, s2, a_top, b_top = [], [], [], []
    for h in range(NH):
        base = h * 2 * PEER_HALF
        s1h = _dot(k1_ref[h], qt[base:base + PEER_HALF])
        s2h = _dot(k2_ref[h], qt[base + PEER_HALF:base + 2 * PEER_HALF])
        s1.append(s1h)
        s2.append(s2h)
        a_top.append(_top_values(s1h, K))
        b_top.append(_top_values(s2h, K))
        r2_ref[h] = _ranks(s2h, b_top[h]).astype(jnp.bfloat16)

    A = [jnp.concatenate([a_top[h][i] for h in range(NH)], axis=0) for i in range(K)]
    Bv = [jnp.concatenate([b_top[h][j] for h in range(NH)], axis=0) for j in range(K)]
    cells = [(i, j) for i in range(K) for j in range(K) if (i + 1) * (j + 1) <= K]
    cand = {(i, j): A[i] + Bv[j] for (i, j) in cells}
    work = list(cand.values())
    for r in range(K):
        m = work[0]
        for cnd in work[1:]:
            m = jnp.maximum(m, cnd)
        if r + 1 < K:
            work = [jnp.where(cnd == m, -jnp.inf, cnd) for cnd in work]
    tau = m
    top = A[0] + Bv[0]
    z = jnp.zeros_like(tau)
    counts = [jnp.zeros_like(tau) for _ in range(K)]
    for (i, j), cnd in cand.items():
        hit = cnd >= tau
        z = z + jnp.where(hit, jnp.exp(cnd - top), 0.0)
        counts[i] = counts[i] + jnp.where(hit, 1.0, 0.0)
    zinv = 0.5 / z

    floors = []
    for c in range(1, K // 2 + 1):
        f = jnp.full_like(tau, jnp.inf)
        for i in range(1, K):
            f = jnp.minimum(f, jnp.where(counts[i] >= float(c), A[i], jnp.inf))
        floors.append(f)

    for h in range(NH):
        cnt = jnp.zeros_like(s1[h])
        for c, f in enumerate(floors):
            cnt = jnp.where(s1[h] >= f[h:h + 1, :], float(c + 1), cnt)
        cnt = jnp.where(s1[h] >= A[0][h:h + 1, :], counts[0][h:h + 1, :], cnt)
        cnt_ref[h] = cnt
        e1_ref[h] = jnp.exp(s1[h] - A[0][h:h + 1, :]) * zinv[h:h + 1, :]
        e2_ref[h] = jnp.exp(s2[h] - Bv[0][h:h + 1, :]).astype(jnp.bfloat16)


def _peer_prep(h1, gain, wqt, k1, k2, *, tt):
    N, D = h1.shape
    NH, NK = k1.shape[0], k1.shape[1]
    full = lambda shape: pl.BlockSpec(shape, lambda i: (0,) * len(shape))
    head_out = pl.BlockSpec((NH, NK, tt), lambda i: (0, 0, i))
    row_shape = jax.ShapeDtypeStruct((NH, NK, N), jnp.float32)
    half_shape = jax.ShapeDtypeStruct((NH, NK, N), jnp.bfloat16)
    return pl.pallas_call(
        _peer_prep_kernel,
        grid=(N // tt,),
        in_specs=[pl.BlockSpec((tt, D), lambda i: (i, 0)), full(gain.shape), full(wqt.shape),
                  full(k1.shape), full(k2.shape)],
        out_specs=(pl.BlockSpec((D, tt), lambda i: (0, i)), head_out, head_out, head_out, head_out),
        out_shape=(jax.ShapeDtypeStruct((D, N), jnp.bfloat16), row_shape, row_shape, half_shape, half_shape),
        compiler_params=pltpu.CompilerParams(
            dimension_semantics=("arbitrary",), vmem_limit_bytes=VMEM_LIMIT),
        name="peer_prep",
    )(h1, gain, wqt, k1, k2)


def _peer_ffn_kernel(u_ref, vt_ref, hft_ref, cnt_ref, e1_ref, r2_ref, e2_ref, h_ref, gain_ref,
                     out_ref, acc, *, eb):
    j = pl.program_id(1)
    NH, NK = PEER_HEADS, PEER_N_KEYS
    tt = hft_ref.shape[1]
    bf = jnp.bfloat16

    @pl.when(j == 0)
    def _():
        acc[...] = jnp.zeros(acc.shape, jnp.float32)

    def row_bcast(ref, h, a):
        return jnp.broadcast_to(ref[h, a:a + 1, :], (NK, tt)).astype(bf)

    def activations(sb):
        return _dot(u_ref[sb * PEER_SUB:(sb + 1) * PEER_SUB, :], hft_ref[...])

    def gates(sb):
        blocks = []
        for a in range(sb * (PEER_SUB // NK), (sb + 1) * (PEER_SUB // NK)):
            g = None
            for h in range(NH):
                hit = r2_ref[h] < row_bcast(cnt_ref, h, a)
                term = jnp.where(hit, e2_ref[h] * row_bcast(e1_ref, h, a), jnp.zeros((), bf))
                g = term if g is None else g + term
            blocks.append(g)
        return jnp.concatenate(blocks, axis=0)

    n_sub = eb // PEER_SUB
    acts = {sb: activations(sb) for sb in range(min(PEER_AHEAD, n_sub))}
    g_cur = gates(0)
    for sb in range(n_sub):
        r0 = sb * PEER_SUB
        if sb + PEER_AHEAD < n_sub:
            acts[sb + PEER_AHEAD] = activations(sb + PEER_AHEAD)
        act = acts.pop(sb)
        act = act.astype(bf)
        gelu2 = act * (1.0 + lax.erf(act * jnp.asarray(math.sqrt(0.5), bf)))
        acc[...] += _dot(vt_ref[:, r0:r0 + PEER_SUB], g_cur * gelu2)
        if sb + 1 < n_sub:
            g_cur = gates(sb + 1)

    @pl.when(j == pl.num_programs(1) - 1)
    def _():
        h2 = h_ref[...] + acc[...].T
        out_ref[...] = _rms_rows(h2, gain_ref[...])


def _peer_ffn(u_bf, vt_bf, hft, cnt, e1, r2, e2, h1, gain, *, tt, eb):
    N, D = h1.shape
    E = u_bf.shape[0]
    NH, NK = PEER_HEADS, PEER_N_KEYS
    ab = eb // NK
    return pl.pallas_call(
        functools.partial(_peer_ffn_kernel, eb=eb),
        grid=(N // tt, E // eb),
        in_specs=[pl.BlockSpec((eb, D), lambda i, j: (j, 0)),
                  pl.BlockSpec((D, eb), lambda i, j: (0, j)),
                  pl.BlockSpec((D, tt), lambda i, j: (0, i)),
                  pl.BlockSpec((NH, ab, tt), lambda i, j: (0, j, i)),
                  pl.BlockSpec((NH, ab, tt), lambda i, j: (0, j, i)),
                  pl.BlockSpec((NH, NK, tt), lambda i, j: (0, 0, i)),
                  pl.BlockSpec((NH, NK, tt), lambda i, j: (0, 0, i)),
                  pl.BlockSpec((tt, D), lambda i, j: (i, 0)),
                  pl.BlockSpec(gain.shape, lambda i, j: (0, 0))],
        out_specs=pl.BlockSpec((tt, D), lambda i, j: (i, 0)),
        out_shape=jax.ShapeDtypeStruct((N, D), jnp.float32),
        scratch_shapes=[pltpu.VMEM((D, tt), jnp.float32)],
        compiler_params=pltpu.CompilerParams(
            dimension_semantics=("arbitrary", "arbitrary"), vmem_limit_bytes=VMEM_LIMIT),
        name="peer_ffn",
    )(u_bf, vt_bf, hft, cnt, e1, r2, e2, h1, gain)


def _tiles(B, S):
    ts_in = min(S, 512)
    ts_gdn = min(S, 512)
    tt = min(B * S, 512)
    eb = 4096
    return ts_in, ts_gdn, tt, eb


def kernel(x, norm_mix_gain, w_in, conv_w, a_log, dt_bias, gdn_out_gain, w_pool, b_pool, pool_scale,
           w_out, norm_ffn_gain, peer_w_query, peer_sub_keys_1, peer_sub_keys_2, peer_expert_u,
           peer_expert_v, norm_final_gain):
    B, S, D = x.shape
    depth = w_in.shape[0]
    pw = pool_scale.shape[1]
    gw = GDN_HEADS * GDN_HEAD_DIM
    H = GDN_HEADS
    ts_in, ts_gdn, tt, eb = _tiles(B, S)
    assert S % ts_in == 0 and S % ts_gdn == 0 and ts_gdn % CHUNK == 0 and (B * S) % tt == 0
    bf = jnp.bfloat16
    sel = _gdn_constants()

    h = x
    for l in range(depth):
        wl = w_in[l]
        wp = wl[:, 0:pw].astype(bf)
        wqkv = wl[:, pw:pw + 3 * gw].astype(bf)
        wz = wl[:, pw + 3 * gw:pw + 4 * gw].astype(bf)
        wba = jnp.pad(wl[:, pw + 4 * gw:pw + 4 * gw + 2 * H], ((0, 0), (0, LANES - 2 * H))).astype(bf)
        alog_pad = jnp.pad(a_log[l], (H, LANES - 2 * H))[None, :]
        dtb_pad = jnp.pad(dt_bias[l], (H, LANES - 2 * H))[None, :]
        pool, q, k, v, gate, bg = _mix_in(
            h, norm_mix_gain[l][None, :], wp, wqkv, wz, wba, conv_w[l], alog_pad, dtb_pad,
            w_pool[l].astype(bf), b_pool[l].reshape(1, pw), pool_scale[l][None, :], ts=ts_in)
        h1 = _gdn_out(h, pool, q, k, v, gate, bg, jnp.tile(gdn_out_gain[l], H)[None, :],
                      w_out[l].astype(bf), sel, ts=ts_gdn)

        h1f = h1.reshape(B * S, D)
        hft, cnt, e1, r2, e2 = _peer_prep(
            h1f, norm_ffn_gain[l][None, :], peer_w_query[l].T.astype(bf),
            peer_sub_keys_1[l], peer_sub_keys_2[l], tt=tt)
        assert depth == 1
        out = _peer_ffn(peer_expert_u[l].astype(bf), peer_expert_v[l].T.astype(bf), hft, cnt, e1, r2, e2,
                        h1f, norm_final_gain[None, :], tt=tt, eb=eb)
        h = out.reshape(B, S, D)
    return h
```

```python
import functools
import math

import jax
import jax.numpy as jnp
import numpy as np
from jax import lax
from jax.experimental import pallas as pl
from jax.experimental.pallas import tpu as pltpu

NORM_EPS = 1e-6
CHUNK = 64
POOL_WINDOWS = (2, 4, 8, 16)
CONV_WIDTH = 4
GDN_HEADS = 4
GDN_HEAD_DIM = 128
PEER_HEADS = 8
PEER_N_KEYS = 128
PEER_HALF = 128
PEER_TOPK = 16
PEER_SUB = 256
PEER_AHEAD = 16

LANES = 128
SUBLANES = 8
POOL_CARRY = 16
CONV_CARRY = 8
VMEM_LIMIT = 62 * 1024 * 1024


def _dot(a, b):
    return jnp.dot(a, b, preferred_element_type=jnp.float32)


def _dot_nt(a, b):
    return lax.dot_general(a, b, (((1,), (1,)), ((), ())), preferred_element_type=jnp.float32)


def _dot_tn(a, b):
    return lax.dot_general(a, b, (((0,), (0,)), ((), ())), preferred_element_type=jnp.float32)


def _rms_rows(x, gain):
    return x * lax.rsqrt(jnp.mean(x * x, axis=-1, keepdims=True) + NORM_EPS) * gain


def _mix_in_kernel(x_ref, gain_ref, wp_ref, wqkv_ref, wz_ref, wba_ref, convw_ref, alog_ref, dtb_ref,
                   wpool_ref, bpool_ref, pscale_ref,
                   pool_ref, q_ref, k_ref, v_ref, gate_ref, bg_ref,
                   pbuf, cbuf, *, ts):
    s = pl.program_id(1)

    @pl.when(s == 0)
    def _():
        pbuf[0:POOL_CARRY, :] = jnp.zeros((POOL_CARRY, pbuf.shape[1]), jnp.float32)
        cbuf[0:CONV_CARRY, :] = jnp.zeros((CONV_CARRY, cbuf.shape[1]), jnp.float32)

    @pl.when(s > 0)
    def _():
        pbuf[0:POOL_CARRY, :] = pbuf[ts:ts + POOL_CARRY, :]
        cbuf[0:CONV_CARRY, :] = cbuf[ts:ts + CONV_CARRY, :]

    x = x_ref[0]
    hn = _rms_rows(x, gain_ref[...]).astype(jnp.bfloat16)

    p = _dot(hn, wp_ref[...])
    pbuf[POOL_CARRY:POOL_CARRY + ts, :] = p
    pos1 = (lax.broadcasted_iota(jnp.int32, (ts, LANES), 0) + s * ts + 1).astype(jnp.float32)
    for g, win in enumerate(POOL_WINDOWS):
        cols = slice(g * LANES, (g + 1) * LANES)
        ext = pbuf[0:POOL_CARRY + ts, cols]
        span = 1
        while span < win:
            ext = ext + pltpu.roll(ext, span, axis=0)
            span *= 2
        ws = ext[POOL_CARRY:POOL_CARRY + ts]
        d = ws / jnp.minimum(pos1, float(win)) - p[:, cols]
        y = _dot(d.astype(jnp.bfloat16), wpool_ref[g]) + bpool_ref[:, cols]
        pool_ref[0, :, cols] = y * pscale_ref[:, cols]

    cbuf[CONV_CARRY:CONV_CARRY + ts, :] = _dot(hn, wqkv_ref[...])
    acc = convw_ref[0:1, :] * cbuf[CONV_CARRY - 3:CONV_CARRY - 3 + ts, :]
    for j in range(1, CONV_WIDTH):
        acc = acc + convw_ref[j:j + 1, :] * cbuf[CONV_CARRY - 3 + j:CONV_CARRY - 3 + j + ts, :]
    qkv = acc * jax.nn.sigmoid(acc)
    width = GDN_HEADS * GDN_HEAD_DIM
    for h in range(GDN_HEADS):
        cq = slice(h * GDN_HEAD_DIM, (h + 1) * GDN_HEAD_DIM)
        ck = slice(width + h * GDN_HEAD_DIM, width + (h + 1) * GDN_HEAD_DIM)
        cv = slice(2 * width + h * GDN_HEAD_DIM, 2 * width + (h + 1) * GDN_HEAD_DIM)
        qh = qkv[:, cq]
        kh = qkv[:, ck]
        q_ref[0, :, cq] = qh * lax.rsqrt(jnp.sum(qh * qh, axis=-1, keepdims=True) + NORM_EPS) * (GDN_HEAD_DIM ** -0.5)
        k_ref[0, :, cq] = kh * lax.rsqrt(jnp.sum(kh * kh, axis=-1, keepdims=True) + NORM_EPS)
        v_ref[0, :, cq] = qkv[:, cv]

    z = _dot(hn, wz_ref[...])
    gate_ref[0] = z * jax.nn.sigmoid(z)

    ba = _dot(hn, wba_ref[...])
    lane = lax.broadcasted_iota(jnp.int32, ba.shape, 1)
    sp_in = ba + dtb_ref[...]
    softplus = jnp.maximum(sp_in, 0.0) + jnp.log1p(jnp.exp(-jnp.abs(sp_in)))
    gdec = -jnp.exp(alog_ref[...]) * softplus
    bg_ref[0] = jnp.where(lane < GDN_HEADS, jax.nn.sigmoid(ba),
                          jnp.where(lane < 2 * GDN_HEADS, gdec, 0.0))


def _mix_in(x, gain, wp, wqkv, wz, wba, convw, alog_pad, dtb_pad, wpool, bpool, pscale, *, ts):
    B, S, D = x.shape
    pw = wp.shape[1]
    gw = wz.shape[1]
    full = lambda shape: pl.BlockSpec(shape, lambda b, s: (0,) * len(shape))
    tile = lambda w: pl.BlockSpec((1, ts, w), lambda b, s: (b, s, 0))
    out_shapes = (
        jax.ShapeDtypeStruct((B, S, pw), jnp.float32),
        jax.ShapeDtypeStruct((B, S, gw), jnp.float32),
        jax.ShapeDtypeStruct((B, S, gw), jnp.float32),
        jax.ShapeDtypeStruct((B, S, gw), jnp.float32),
        jax.ShapeDtypeStruct((B, S, gw), jnp.float32),
        jax.ShapeDtypeStruct((B, S, LANES), jnp.float32),
    )
    return pl.pallas_call(
        functools.partial(_mix_in_kernel, ts=ts),
        grid=(B, S // ts),
        in_specs=[tile(D), full(gain.shape), full(wp.shape), full(wqkv.shape), full(wz.shape),
                  full(wba.shape), full(convw.shape), full(alog_pad.shape), full(dtb_pad.shape),
                  full(wpool.shape), full(bpool.shape), full(pscale.shape)],
        out_specs=(tile(pw), tile(gw), tile(gw), tile(gw), tile(gw), tile(LANES)),
        out_shape=out_shapes,
        scratch_shapes=[pltpu.VMEM((POOL_CARRY + ts, pw), jnp.float32),
                        pltpu.VMEM((CONV_CARRY + ts, wqkv.shape[1]), jnp.float32)],
        compiler_params=pltpu.CompilerParams(
            dimension_semantics=("arbitrary", "arbitrary"), vmem_limit_bytes=VMEM_LIMIT),
        name="mix_in",
    )(x, gain, wp, wqkv, wz, wba, convw, alog_pad, dtb_pad, wpool, bpool, pscale)


def _split3(a):
    p1 = a.astype(jnp.bfloat16)
    r1 = a - p1.astype(jnp.float32)
    p2 = r1.astype(jnp.bfloat16)
    p3 = (r1 - p2.astype(jnp.float32)).astype(jnp.bfloat16)
    return p1, p2, p3


def _gdn_out_kernel(x_ref, pool_ref, q_ref, k_ref, v_ref, gate_ref, bg_ref, ogain_ref, wout_ref,
                    sel_ref,
                    h_ref, state, o_buf, *, ts, nb):
    s = pl.program_id(1)
    H, Dh, C = GDN_HEADS, GDN_HEAD_DIM, CHUNK
    R = H * C
    bf = jnp.bfloat16

    @pl.when(s == 0)
    def _():
        state[...] = jnp.zeros(state.shape, jnp.float32)

    ri = lax.broadcasted_iota(jnp.int32, (R, R), 0)
    ci = lax.broadcasted_iota(jnp.int32, (R, R), 1)
    same = (ri // C) == (ci // C)
    causal = same & (ri >= ci)
    strict = same & (ri > ci)
    eye = (ri == ci).astype(jnp.float32)
    trow = lax.broadcasted_iota(jnp.int32, (ts, LANES), 0) % C
    tlane = lax.broadcasted_iota(jnp.int32, (ts, LANES), 1)
    lane = lax.broadcasted_iota(jnp.int32, (C, LANES), 1)
    n_chunks = ts // C

    def stacked(ref, b, rows):
        return jnp.concatenate([ref[b, rows, h * Dh:(h + 1) * Dh] for h in range(H)], axis=0)

    pre = []
    for b in range(nb):
        bg = bg_ref[b]
        cum = bg
        step = 1
        while step < C:
            cum = cum + jnp.where(trow >= step, pltpu.roll(cum, step, axis=0), 0.0)
            step *= 2
        tot = jnp.concatenate(
            [jnp.broadcast_to(cum[c * C + C - 1:c * C + C, :], (C, LANES)) for c in range(n_chunks)], axis=0)
        feat = jnp.where(tlane < H, bg,
                         jnp.where(tlane < 2 * H, cum,
                                   jnp.where(tlane < 3 * H, pltpu.roll(tot, H, axis=1), 0.0)))
        for c in range(n_chunks):
            rows = slice(c * C, (c + 1) * C)
            fc = feat[rows]
            fm = jnp.concatenate([jnp.where(lane % H == h, fc, 0.0) for h in range(H)], axis=0)
            fcat = jnp.concatenate(_split3(fm), axis=1)
            bc = _dot(fcat, sel_ref[...])
            gcum_cols = bc[:, 0:R].T
            pre.append(dict(b=b, c=c, rows=rows, bc=bc, gcum_cols=gcum_cols))
    for p in pre:
        bc = p["bc"]
        p["gcum"] = bc[:, 0:Dh]
        p["gtot"] = bc[:, R + Dh:R + 2 * Dh]
        diff = bc[:, 0:R] - p["gcum_cols"]
        p["dec"] = jnp.where(causal, jnp.exp(jnp.where(causal, diff, 0.0)), 0.0)
        beta_b = bc[:, R:R + Dh]
        ks = stacked(k_ref, p["b"], p["rows"])
        p["ks"] = ks
        p["ks_bf"] = ks.astype(bf)
        p["kb"] = ks * beta_b
        p["vb"] = stacked(v_ref, p["b"], p["rows"]) * beta_b
        kk = _dot_nt(p["kb"].astype(bf), p["ks_bf"])
        m = jnp.where(strict, -(kk * p["dec"]), 0.0)
        p["tinv"] = eye + m
        p["pw"] = m
    for _ in range(int(math.log2(C)) - 1):
        for p in pre:
            pw_bf = p["pw"].astype(bf)
            p["pw"] = _dot(pw_bf, pw_bf)
        for p in pre:
            p["tinv"] = p["tinv"] + _dot(p["pw"].astype(bf), p["tinv"].astype(bf))
    for p in pre:
        egc = jnp.exp(p["gcum"])
        sol = _dot(p["tinv"].astype(bf), jnp.concatenate([p["vb"], p["kb"] * egc], axis=1).astype(bf))
        qs = stacked(q_ref, p["b"], p["rows"])
        p["u"] = sol[:, 0:Dh]
        p["w"] = sol[:, Dh:2 * Dh].astype(bf)
        p["qk"] = (_dot_nt(qs.astype(bf), p["ks_bf"]) * p["dec"]).astype(bf)
        p["qd"] = (qs * egc).astype(bf)
        p["kd"] = (p["ks"] * jnp.exp(p["gtot"] - p["gcum"])).astype(bf)
        p["cd"] = jnp.exp(p["gtot"])

    st = [[state[b, h] for h in range(H)] for b in range(nb)]
    for c in range(n_chunks):
        for b in range(nb):
            p = pre[b * n_chunks + c]
            st_bf = [t.astype(bf) for t in st[b]]
            u, w, qd, kd, cd = p["u"], p["w"], p["qd"], p["kd"], p["cd"]
            vnew = jnp.concatenate(
                [u[h * C:(h + 1) * C] - _dot(w[h * C:(h + 1) * C], st_bf[h]) for h in range(H)], axis=0)
            vnew_bf = vnew.astype(bf)
            o_intra = _dot(p["qk"], vnew_bf)
            for h in range(H):
                hr = slice(h * C, (h + 1) * C)
                o_buf[b, p["rows"], h * Dh:(h + 1) * Dh] = _dot(qd[hr], st_bf[h]) + o_intra[hr]
                st[b][h] = st[b][h] * cd[h * C:h * C + 1, :] + _dot_tn(kd[hr], vnew_bf[hr])
    for b in range(nb):
        for h in range(H):
            state[b, h] = st[b][h]

    for b in range(nb):
        o = o_buf[b]
        gdn = jnp.concatenate(
            [o[:, h * Dh:(h + 1) * Dh]
             * lax.rsqrt(jnp.mean(o[:, h * Dh:(h + 1) * Dh] * o[:, h * Dh:(h + 1) * Dh], axis=-1, keepdims=True) + NORM_EPS)
             for h in range(H)], axis=1)
        gdn = gdn * ogain_ref[...] * gate_ref[b]
        mixed = jnp.concatenate([pool_ref[b], gdn], axis=1).astype(jnp.bfloat16)
        h_ref[b] = x_ref[b] + _dot(mixed, wout_ref[...])


def _gdn_out(x, pool, q, k, v, gate, bg, ogain, wout, sel, *, ts, nb):
    B, S, D = x.shape
    gw = q.shape[2]
    full = lambda shape: pl.BlockSpec(shape, lambda b, s: (0,) * len(shape))
    tile = lambda w: pl.BlockSpec((nb, ts, w), lambda b, s: (b, s, 0))
    return pl.pallas_call(
        functools.partial(_gdn_out_kernel, ts=ts, nb=nb),
        grid=(B // nb, S // ts),
        in_specs=[tile(D), tile(pool.shape[2]), tile(gw), tile(gw), tile(gw), tile(gw), tile(LANES),
                  full(ogain.shape), full(wout.shape), full(sel.shape)],
        out_specs=tile(D),
        out_shape=jax.ShapeDtypeStruct((B, S, D), jnp.float32),
        scratch_shapes=[pltpu.VMEM((nb, GDN_HEADS, GDN_HEAD_DIM, GDN_HEAD_DIM), jnp.float32),
                        pltpu.VMEM((nb, ts, gw), jnp.float32)],
        compiler_params=pltpu.CompilerParams(
            dimension_semantics=("arbitrary", "arbitrary"), vmem_limit_bytes=VMEM_LIMIT),
        name="gdn_out",
    )(x, pool, q, k, v, gate, bg, ogain, wout, sel)


def _gdn_constants():
    H, C, Dh = GDN_HEADS, CHUNK, GDN_HEAD_DIM
    R = H * C
    sel = np.zeros((3 * LANES, R + 2 * Dh), np.float32)
    for piece in range(3):
        base = piece * LANES
        sel[base + H:base + 2 * H, 0:R] = 1.0
        sel[base:base + H, R:R + Dh] = 1.0
        sel[base + 2 * H:base + 3 * H, R + Dh:R + 2 * Dh] = 1.0
    return jnp.asarray(sel, jnp.bfloat16)


def _sorting_network(n):
    def merge(lo, hi, r):
        step = r * 2
        if step < hi - lo:
            yield from merge(lo, hi, step)
            yield from merge(lo + r, hi, step)
            yield from ((i, i + r) for i in range(lo + r, hi - r, step))
        else:
            yield (lo, lo + r)

    def sort(lo, hi):
        if hi - lo >= 1:
            mid = lo + (hi - lo) // 2
            yield from sort(lo, mid)
            yield from sort(mid + 1, hi)
            yield from merge(lo, hi, 1)

    return tuple(sort(0, n - 1))


def _top_values(x, n):
    lists = [x[i * SUBLANES:(i + 1) * SUBLANES] for i in range(x.shape[0] // SUBLANES)]
    for i, j in _sorting_network(len(lists)):
        lists[i], lists[j] = jnp.maximum(lists[i], lists[j]), jnp.minimum(lists[i], lists[j])
    tops = []
    for r in range(n):
        m = jnp.max(lists[0], axis=0, keepdims=True)
        tops.append(m)
        depth = min(n - 1 - r, len(lists))
        if depth:
            hit = lists[0] == m
            for i in range(depth):
                nxt = lists[i + 1] if i + 1 < len(lists) else -jnp.inf
                lists[i] = jnp.where(hit, nxt, lists[i])
    return tops


def _ranks(x, tops):
    rank = jnp.zeros(x.shape, jnp.float32)
    for r, m in enumerate(tops):
        rank = jnp.where(x < m, float(r + 1), rank)
    return rank


def _peer_prep_kernel(h_ref, gain_ref, wqt_ref, k1_ref, k2_ref,
                      hft_ref, cnt_ref, e1_ref, r2_ref, e2_ref):
    NH, K = PEER_HEADS, PEER_TOPK
    hf = _rms_rows(h_ref[...], gain_ref[...])
    hft = hf.T.astype(jnp.bfloat16)
    hft_ref[...] = hft
    qt = _dot(wqt_ref[...], hft)

    s1, s2, a_top, b_top = [], [], [], []
    for h in range(NH):
        base = h * 2 * PEER_HALF
        s1h = _dot(k1_ref[h], qt[base:base + PEER_HALF])
        s2h = _dot(k2_ref[h], qt[base + PEER_HALF:base + 2 * PEER_HALF])
        s1.append(s1h)
        s2.append(s2h)
        a_top.append(_top_values(s1h, K))
        b_top.append(_top_values(s2h, K))
        r2_ref[h] = _ranks(s2h, b_top[h]).astype(jnp.bfloat16)

    A = [jnp.concatenate([a_top[h][i] for h in range(NH)], axis=0) for i in range(K)]
    Bv = [jnp.concatenate([b_top[h][j] for h in range(NH)], axis=0) for j in range(K)]
    cells = [(i, j) for i in range(K) for j in range(K) if (i + 1) * (j + 1) <= K]
    cand = {(i, j): A[i] + Bv[j] for (i, j) in cells}
    work = list(cand.values())
    for r in range(K):
        m = work[0]
        for cnd in work[1:]:
            m = jnp.maximum(m, cnd)
        if r + 1 < K:
            work = [jnp.where(cnd == m, -jnp.inf, cnd) for cnd in work]
    tau = m
    top = A[0] + Bv[0]
    z = jnp.zeros_like(tau)
    counts = [jnp.zeros_like(tau) for _ in range(K)]
    for (i, j), cnd in cand.items():
        hit = cnd >= tau
        z = z + jnp.where(hit, jnp.exp(cnd - top), 0.0)
        counts[i] = counts[i] + jnp.where(hit, 1.0, 0.0)
    zinv = 0.5 / z

    floors = []
    for c in range(1, K // 2 + 1):
        f = jnp.full_like(tau, jnp.inf)
        for i in range(1, K):
            f = jnp.minimum(f, jnp.where(counts[i] >= float(c), A[i], jnp.inf))
        floors.append(f)

    for h in range(NH):
        cnt = jnp.zeros_like(s1[h])
        for c, f in enumerate(floors):
            cnt = jnp.where(s1[h] >= f[h:h + 1, :], float(c + 1), cnt)
        cnt = jnp.where(s1[h] >= A[0][h:h + 1, :], counts[0][h:h + 1, :], cnt)
        cnt_ref[h] = cnt
        e1_ref[h] = jnp.exp(s1[h] - A[0][h:h + 1, :]) * zinv[h:h + 1, :]
        e2_ref[h] = jnp.exp(s2[h] - Bv[0][h:h + 1, :]).astype(jnp.bfloat16)


def _peer_prep(h1, gain, wqt, k1, k2, *, tt):
    N, D = h1.shape
    NH, NK = k1.shape[0], k1.shape[1]
    full = lambda shape: pl.BlockSpec(shape, lambda i: (0,) * len(shape))
    head_out = pl.BlockSpec((NH, NK, tt), lambda i: (0, 0, i))
    row_shape = jax.ShapeDtypeStruct((NH, NK, N), jnp.float32)
    half_shape = jax.ShapeDtypeStruct((NH, NK, N), jnp.bfloat16)
    return pl.pallas_call(
        _peer_prep_kernel,
        grid=(N // tt,),
        in_specs=[pl.BlockSpec((tt, D), lambda i: (i, 0)), full(gain.shape), full(wqt.shape),
                  full(k1.shape), full(k2.shape)],
        out_specs=(pl.BlockSpec((D, tt), lambda i: (0, i)), head_out, head_out, head_out, head_out),
        out_shape=(jax.ShapeDtypeStruct((D, N), jnp.bfloat16), row_shape, row_shape, half_shape, half_shape),
        compiler_params=pltpu.CompilerParams(
            dimension_semantics=("arbitrary",), vmem_limit_bytes=VMEM_LIMIT),
        name="peer_prep",
    )(h1, gain, wqt, k1, k2)


def _peer_ffn_kernel(u_ref, vt_ref, hft_ref, cnt_ref, e1_ref, r2_ref, e2_ref, h_ref, gain_ref,
                     out_ref, acc, *, eb):
    j = pl.program_id(1)
    NH, NK = PEER_HEADS, PEER_N_KEYS
    tt = hft_ref.shape[1]
    bf = jnp.bfloat16

    @pl.when(j == 0)
    def _():
        acc[...] = jnp.zeros(acc.shape, jnp.float32)

    def row_bcast(ref, h, a):
        return jnp.broadcast_to(ref[h, a:a + 1, :], (NK, tt)).astype(bf)

    def activations(sb):
        return _dot(u_ref[sb * PEER_SUB:(sb + 1) * PEER_SUB, :], hft_ref[...])

    def gates(sb):
        blocks = []
        for a in range(sb * (PEER_SUB // NK), (sb + 1) * (PEER_SUB // NK)):
            g = None
            for h in range(NH):
                hit = r2_ref[h] < row_bcast(cnt_ref, h, a)
                term = jnp.where(hit, e2_ref[h] * row_bcast(e1_ref, h, a), jnp.zeros((), bf))
                g = term if g is None else g + term
            blocks.append(g)
        return jnp.concatenate(blocks, axis=0)

    n_sub = eb // PEER_SUB
    acts = {sb: activations(sb) for sb in range(min(PEER_AHEAD, n_sub))}
    g_cur = gates(0)
    for sb in range(n_sub):
        r0 = sb * PEER_SUB
        if sb + PEER_AHEAD < n_sub:
            acts[sb + PEER_AHEAD] = activations(sb + PEER_AHEAD)
        act = acts.pop(sb)
        act = act.astype(bf)
        gelu2 = act * (1.0 + lax.erf(act * jnp.asarray(math.sqrt(0.5), bf)))
        acc[...] += _dot(vt_ref[:, r0:r0 + PEER_SUB], g_cur * gelu2)
        if sb + 1 < n_sub:
            g_cur = gates(sb + 1)

    @pl.when(j == pl.num_programs(1) - 1)
    def _():
        h2 = h_ref[...] + acc[...].T
        out_ref[...] = _rms_rows(h2, gain_ref[...])


def _peer_ffn(u_bf, vt_bf, hft, cnt, e1, r2, e2, h1, gain, *, tt, eb):
    N, D = h1.shape
    E = u_bf.shape[0]
    NH, NK = PEER_HEADS, PEER_N_KEYS
    ab = eb // NK
    return pl.pallas_call(
        functools.partial(_peer_ffn_kernel, eb=eb),
        grid=(N // tt, E // eb),
        in_specs=[pl.BlockSpec((eb, D), lambda i, j: (j, 0)),
                  pl.BlockSpec((D, eb), lambda i, j: (0, j)),
                  pl.BlockSpec((D, tt), lambda i, j: (0, i)),
                  pl.BlockSpec((NH, ab, tt), lambda i, j: (0, j, i)),
                  pl.BlockSpec((NH, ab, tt), lambda i, j: (0, j, i)),
                  pl.BlockSpec((NH, NK, tt), lambda i, j: (0, 0, i)),
                  pl.BlockSpec((NH, NK, tt), lambda i, j: (0, 0, i)),
                  pl.BlockSpec((tt, D), lambda i, j: (i, 0)),
                  pl.BlockSpec(gain.shape, lambda i, j: (0, 0))],
        out_specs=pl.BlockSpec((tt, D), lambda i, j: (i, 0)),
        out_shape=jax.ShapeDtypeStruct((N, D), jnp.float32),
        scratch_shapes=[pltpu.VMEM((D, tt), jnp.float32)],
        compiler_params=pltpu.CompilerParams(
            dimension_semantics=("arbitrary", "arbitrary"), vmem_limit_bytes=VMEM_LIMIT),
        name="peer_ffn",
    )(u_bf, vt_bf, hft, cnt, e1, r2, e2, h1, gain)


def _tiles(B, S):
    ts_in = min(S, 512)
    ts_gdn = min(S, 256)
    nb_gdn = 2 if B % 2 == 0 else 1
    tt = min(B * S, 512)
    eb = 4096
    return ts_in, ts_gdn, nb_gdn, tt, eb


def kernel(x, norm_mix_gain, w_in, conv_w, a_log, dt_bias, gdn_out_gain, w_pool, b_pool, pool_scale,
           w_out, norm_ffn_gain, peer_w_query, peer_sub_keys_1, peer_sub_keys_2, peer_expert_u,
           peer_expert_v, norm_final_gain):
    B, S, D = x.shape
    depth = w_in.shape[0]
    pw = pool_scale.shape[1]
    gw = GDN_HEADS * GDN_HEAD_DIM
    H = GDN_HEADS
    ts_in, ts_gdn, nb_gdn, tt, eb = _tiles(B, S)
    assert S % ts_in == 0 and S % ts_gdn == 0 and ts_gdn % CHUNK == 0 and (B * S) % tt == 0
    bf = jnp.bfloat16
    sel = _gdn_constants()

    h = x
    for l in range(depth):
        wl = w_in[l]
        wp = wl[:, 0:pw].astype(bf)
        wqkv = wl[:, pw:pw + 3 * gw].astype(bf)
        wz = wl[:, pw + 3 * gw:pw + 4 * gw].astype(bf)
        wba = jnp.pad(wl[:, pw + 4 * gw:pw + 4 * gw + 2 * H], ((0, 0), (0, LANES - 2 * H))).astype(bf)
        alog_pad = jnp.pad(a_log[l], (H, LANES - 2 * H))[None, :]
        dtb_pad = jnp.pad(dt_bias[l], (H, LANES - 2 * H))[None, :]
        pool, q, k, v, gate, bg = _mix_in(
            h, norm_mix_gain[l][None, :], wp, wqkv, wz, wba, conv_w[l], alog_pad, dtb_pad,
            w_pool[l].astype(bf), b_pool[l].reshape(1, pw), pool_scale[l][None, :], ts=ts_in)
        h1 = _gdn_out(h, pool, q, k, v, gate, bg, jnp.tile(gdn_out_gain[l], H)[None, :],
                      w_out[l].astype(bf), sel, ts=ts_gdn, nb=nb_gdn)

        h1f = h1.reshape(B * S, D)
        hft, cnt, e1, r2, e2 = _peer_prep(
            h1f, norm_ffn_gain[l][None, :], peer_w_query[l].T.astype(bf),
            peer_sub_keys_1[l], peer_sub_keys_2[l], tt=tt)
        assert depth == 1
        out = _peer_ffn(peer_expert_u[l].astype(bf), peer_expert_v[l].T.astype(bf), hft, cnt, e1, r2, e2,
                        h1f, norm_final_gain[None, :], tt=tt, eb=eb)
        h = out.reshape(B, S, D)
    return h
```
